```python
import jax, jax.numpy as jnp
from jax import lax
import numpy as np

D_MODEL = 1024
BATCH = 8
SEQ = 2048
DEPTH = 1
DEC_BATCH = 128
DEC_SEQ = 4
PAST_LEN = 16384
PAGE_SIZE = 128

H_A = 8
DK_A = 128
DV_A = 64
D_QA = H_A * DK_A
D_VA = H_A * DV_A
HGRN_CHUNK = 64
H_B = 4
CH_B = 128
D_B = H_B * CH_B
CHUNK = 128
D_FF = 2816
CONV_W = 3
EPS = 1e-6

SIZES = (D_QA, D_QA, D_VA, D_VA, D_B, D_B, D_MODEL, D_MODEL)
D_IN = sum(SIZES)
SPLIT_IDX = tuple(int(s) for s in np.cumsum(SIZES)[:-1])

kernel_name = 'hgrn2_gmlp_convffn_hybrid_step'


def rms_norm(x, g):
    xf = x.astype(jnp.float32)
    y = xf * lax.rsqrt(jnp.mean(xf * xf, axis=-1, keepdims=True) + EPS)
    return (y * g.astype(jnp.float32)).astype(x.dtype)


def layer_norm(x, g, b):
    xf = x.astype(jnp.float32)
    mu = jnp.mean(xf, axis=-1, keepdims=True)
    xc = xf - mu
    y = xc * lax.rsqrt(jnp.mean(xc * xc, axis=-1, keepdims=True) + EPS)
    return (y * g.astype(jnp.float32) + b.astype(jnp.float32)).astype(x.dtype)


def hgrn2_chunk(S, q, k, i, logf):
    C = q.shape[-2]
    b = jnp.cumsum(logf, axis=-2)
    causal = jnp.tril(jnp.ones((C, C), dtype=bool))
    diff = b[..., :, None, :] - b[..., None, :, :]
    decay = jnp.exp(jnp.where(causal[:, :, None], diff, -jnp.inf))
    scores = jnp.einsum('bhtk,bhtsk,bhsk->bhts', q, decay, k)
    o = jnp.einsum('bhts,bhsv->bhtv', scores, i) + jnp.einsum('bhtk,bhkv->bhtv', q * jnp.exp(b), S)
    b_last = b[..., -1:, :]
    S_new = jnp.exp(b_last[..., 0, :])[..., None] * S + jnp.einsum('bhsk,bhsv->bhkv', k * jnp.exp(b_last - b), i)
    return S_new, o


def hgrn2_mixer(q, f_logit, i, S0, lb):
    B, L, _ = q.shape
    qf = jax.nn.silu(q.astype(jnp.float32))
    f = lb + (1.0 - lb) * jax.nn.sigmoid(f_logit.astype(jnp.float32))
    logf = jnp.log(f)
    k = 1.0 - f
    heads = lambda t, d: t.reshape(B, L, H_A, d).transpose(0, 2, 1, 3)
    qh, kh, lfh = heads(qf, DK_A), heads(k, DK_A), heads(logf, DK_A)
    ih = heads(i.astype(jnp.float32), DV_A)
    C = HGRN_CHUNK if L % HGRN_CHUNK == 0 else L
    nc = L // C
    to_chunks = lambda t: jnp.moveaxis(t.reshape(B, H_A, nc, C, t.shape[-1]), 2, 0)
    S_fin, o = lax.scan(lambda S, xs: hgrn2_chunk(S, *xs), S0.astype(jnp.float32),
                        (to_chunks(qh), to_chunks(kh), to_chunks(ih), to_chunks(lfh)))
    o = jnp.moveaxis(o, 0, 2).reshape(B, H_A, L, DV_A).transpose(0, 2, 1, 3)
    return o, S_fin


def chunk_spatial_gate(u, vn, w_s, b_s):
    B, L, _ = u.shape
    c = min(L, CHUNK)
    n = L // c
    vg = vn.reshape(B, n, c, H_B, CH_B)
    w = jnp.tril(w_s[:, :c, :c])
    s = jnp.einsum('gts,bnsgc->bntgc', w, vg) + b_s[:, :c].T[None, None, :, :, None]
    return u * s.reshape(B, L, D_B)


def decoder_layer(x, S0, conv_prev, lb, mix_pre_g, w_in, hgrn_norm_g, gmlp_ln_g, gmlp_ln_b, w_s, b_s,
                  w_pa, w_pb, w_o, mix_post_g, ffn_pre_g, w_up, conv_w, conv_b, w_down, ffn_post_g):
    B, L, _ = x.shape
    xn = rms_norm(x, mix_pre_g)
    z = xn @ w_in
    q, f_logit, i, og, u, v, ga, gb = jnp.split(z, SPLIT_IDX, axis=-1)
    oa, S_new = hgrn2_mixer(q, f_logit, i, S0, lb)
    oa = rms_norm(oa.astype(x.dtype), hgrn_norm_g) * jax.nn.silu(og.reshape(B, L, H_A, DV_A))
    oa = oa.reshape(B, L, D_VA)
    vn = layer_norm(jax.nn.gelu(v), gmlp_ln_g, gmlp_ln_b)
    ob = chunk_spatial_gate(jax.nn.gelu(u), vn, w_s, b_s)
    h = jax.nn.sigmoid(ga) * (oa @ w_pa) + jax.nn.sigmoid(gb) * (ob @ w_pb)
    x = x + rms_norm(h @ w_o, mix_post_g)
    xn = rms_norm(x, ffn_pre_g)
    up = xn @ w_up
    hp = jnp.concatenate([conv_prev.astype(up.dtype), up], axis=1)
    conv = conv_b + sum(conv_w[j] * hp[:, j:j + L] for j in range(CONV_W))
    gate, val = jnp.split(conv, 2, axis=-1)
    x = x + rms_norm((jax.nn.gelu(gate) * val) @ w_down, ffn_post_g)
    return x, S_new, hp[:, -(CONV_W - 1):], vn


def setup_inputs(seed: int = 0) -> dict:
    key = jax.random.key(seed)
    ks = jax.random.split(key, 24)
    nrm = lambda k, shape, s: jax.random.normal(k, shape, jnp.float32) * s
    gain = lambda k, shape: 1.0 + 0.05 * jax.random.normal(k, shape, jnp.float32)
    return {
        'x_prompt': nrm(ks[0], (BATCH, SEQ, D_MODEL), 1.0),
        'x_sample': nrm(ks[1], (DEC_BATCH, DEC_SEQ, D_MODEL), 1.0),
        'state_hgrn': nrm(ks[2], (DEPTH, DEC_BATCH, H_A, DK_A, DV_A), 0.5),
        'cache_ffn_conv': nrm(ks[3], (DEPTH, DEC_BATCH, CONV_W - 1, 2 * D_FF), 1.0),
        'lb_param': nrm(ks[4], (DEPTH + 1, D_QA), 0.1),
        'mix_pre_g': gain(ks[5], (DEPTH, D_MODEL)),
        'w_in': nrm(ks[6], (DEPTH, D_MODEL, D_IN), D_MODEL ** -0.5),
        'hgrn_norm_g': gain(ks[7], (DEPTH, DV_A)),
        'gmlp_ln_g': gain(ks[8], (DEPTH, D_B)),
        'gmlp_ln_b': nrm(ks[9], (DEPTH, D_B), 0.02),
        'w_s': nrm(ks[10], (DEPTH, H_B, CHUNK, CHUNK), CHUNK ** -0.5),
        'b_s': 1.0 + nrm(ks[11], (DEPTH, H_B, CHUNK), 0.1),
        'w_pa': nrm(ks[12], (DEPTH, D_VA, D_MODEL), D_VA ** -0.5),
        'w_pb': nrm(ks[13], (DEPTH, D_B, D_MODEL), D_B ** -0.5),
        'w_o': nrm(ks[14], (DEPTH, D_MODEL, D_MODEL), D_MODEL ** -0.5),
        'mix_post_g': gain(ks[15], (DEPTH, D_MODEL)),
        'ffn_pre_g': gain(ks[16], (DEPTH, D_MODEL)),
        'w_up': nrm(ks[17], (DEPTH, D_MODEL, 2 * D_FF), D_MODEL ** -0.5),
        'conv_w': nrm(ks[18], (DEPTH, CONV_W, 2 * D_FF), CONV_W ** -0.5),
        'conv_b': nrm(ks[19], (DEPTH, 2 * D_FF), 0.02),
        'w_down': nrm(ks[20], (DEPTH, D_FF, D_MODEL), D_FF ** -0.5),
        'ffn_post_g': gain(ks[21], (DEPTH, D_MODEL)),
    }


def reference(x_prompt, x_sample, state_hgrn, cache_ffn_conv, lb_param, mix_pre_g, w_in, hgrn_norm_g,
              gmlp_ln_g, gmlp_ln_b, w_s, b_s, w_pa, w_pb, w_o, mix_post_g, ffn_pre_g, w_up, conv_w,
              conv_b, w_down, ffn_post_g):
    lb_all = jnp.cumsum(jax.nn.softmax(lb_param.astype(jnp.float32), axis=0), axis=0)
    yp, ys = x_prompt, x_sample
    sp_l, ss_l, cp_l, cs_l, vs_l = [], [], [], [], []
    S_zero = jnp.zeros((x_prompt.shape[0], H_A, DK_A, DV_A), jnp.float32)
    conv_zero = jnp.zeros((x_prompt.shape[0], CONV_W - 1, 2 * D_FF), x_prompt.dtype)
    for l in range(DEPTH):
        w = (mix_pre_g[l], w_in[l], hgrn_norm_g[l], gmlp_ln_g[l], gmlp_ln_b[l], w_s[l], b_s[l],
             w_pa[l], w_pb[l], w_o[l], mix_post_g[l], ffn_pre_g[l], w_up[l], conv_w[l], conv_b[l],
             w_down[l], ffn_post_g[l])
        yp, sp, cp, _ = decoder_layer(yp, S_zero, conv_zero, lb_all[l], *w)
        ys, ss, cs, vs = decoder_layer(ys, state_hgrn[l], cache_ffn_conv[l], lb_all[l], *w)
        sp_l.append(sp); ss_l.append(ss); cp_l.append(cp); cs_l.append(cs); vs_l.append(vs)
    return (yp, ys, jnp.stack(sp_l), jnp.stack(ss_l), jnp.stack(cp_l), jnp.stack(cs_l), jnp.stack(vs_l))
```

```python
import functools

import jax
import jax.numpy as jnp
from jax import lax
from jax.experimental import pallas as pl
from jax.experimental.pallas import tpu as pltpu

F32 = jnp.float32
BF16 = jnp.bfloat16

D_MODEL = 1024
H_A, DK_A, DV_A = 8, 128, 64
D_QA, D_VA = H_A * DK_A, H_A * DV_A
H_B, CH_B = 4, 128
D_B = H_B * CH_B
GMLP_CHUNK = 128
D_FF = 2816
CONV_W = 3
EPS = 1e-6

OFF_Q, OFF_F, OFF_I, OFF_OG, OFF_U, OFF_V, OFF_GA, OFF_GB = 0, 1024, 2048, 2560, 3072, 3584, 4096, 5120
D_IN = 6144

HGRN_C = 64
PAIR_K = 2 * DK_A
PAIR_V = 2 * DV_A
N_PAIR = H_A // 2
T_MIX = 256
T_FFN = 512
FFN_CB = 256
SAMPLE_BB = 8
VMEM_LIMIT = 56 * 1024 * 1024


def _dot(a, b):
    return jnp.dot(a.astype(BF16), b.astype(BF16), preferred_element_type=F32)


def _dot_nt(a, b):
    return lax.dot_general(a.astype(BF16), b.astype(BF16), (((1,), (1,)), ((), ())),
                           preferred_element_type=F32)


def _dot_tn(a, b):
    return lax.dot_general(a.astype(BF16), b.astype(BF16), (((0,), (0,)), ((), ())),
                           preferred_element_type=F32)


def _dot_split(m, x):
    hi = x.astype(BF16)
    lo = (x - hi.astype(F32)).astype(BF16)
    m = m.astype(BF16)
    return (jnp.dot(m, hi, preferred_element_type=F32) + jnp.dot(m, lo, preferred_element_type=F32))


def _rms(x, g):
    return x * lax.rsqrt(jnp.mean(x * x, axis=-1, keepdims=True) + EPS) * g


def _layer_norm(x, g, b):
    xc = x - jnp.mean(x, axis=-1, keepdims=True)
    return xc * lax.rsqrt(jnp.mean(xc * xc, axis=-1, keepdims=True) + EPS) * g + b


def _lower_bound(lbp):
    e = jnp.exp(lbp - jnp.max(lbp, axis=0, keepdims=True))
    return e[0:1] / jnp.sum(e, axis=0, keepdims=True)


def _hgrn_features(q, f_logit, lb):
    f = lb + (1.0 - lb) * jax.nn.sigmoid(f_logit)
    return q * jax.nn.sigmoid(q), jnp.log(f), 1.0 - f


def _iota(shape, dim):
    return lax.broadcasted_iota(jnp.int32, shape, dim)


def _div(x, n):
    assert n & (n - 1) == 0
    return lax.shift_right_logical(x, n.bit_length() - 1)


def _mod(x, n):
    assert n & (n - 1) == 0
    return x & (n - 1)


def _ones_where(cond):
    return jnp.where(cond, 1.0, 0.0).astype(BF16)


def _head_mean_square(o):
    same_head = _ones_where(_div(_iota((D_VA, D_VA), 0), DV_A) == _div(_iota((D_VA, D_VA), 1), DV_A))
    return _dot_split_rhs(o * o, same_head) * (1.0 / DV_A)


def _dot_split_rhs(x, m):
    hi = x.astype(BF16)
    lo = (x - hi.astype(F32)).astype(BF16)
    return (jnp.dot(hi, m, preferred_element_type=F32) + jnp.dot(lo, m, preferred_element_type=F32))


def _mix_out(x, o, og, ga_sig, hb, hng, wpa_ref, wo_ref, gpost):
    oa = o * lax.rsqrt(_head_mean_square(o) + EPS) * hng * (og * jax.nn.sigmoid(og))
    h = ga_sig * _dot(oa, wpa_ref[...]) + hb
    return x + _rms(_dot(h, wo_ref[...]), gpost)


def _prompt_mixer_kernel(x_ref, lbp_ref, gpre_ref, win_ref, hng_ref, lng_ref, lnb_ref, ws_ref, bsb_ref,
                         wpa_ref, wpb_ref, wo_ref, gpost_ref, y_ref, sp_ref, s_ref, o_ref, ob_ref):
    j = pl.program_id(1)
    t_tile = x_ref.shape[1]
    c = HGRN_C

    @pl.when(j == 0)
    def _():
        s_ref[...] = jnp.zeros_like(s_ref)

    x = x_ref[0]
    xn = _rms(x, gpre_ref[...]).astype(BF16)
    lb = _lower_bound(lbp_ref[...])

    def proj(off, width):
        return jnp.dot(xn, win_ref[:, off:off + width], preferred_element_type=F32)

    r, s = _iota((c + 8, c), 0), _iota((c + 8, c), 1)
    cum_m = jnp.where(r < c, jnp.where(s <= r, 1.0, 0.0) - jnp.where(s < c // 2, 1.0, 0.0), 1.0)
    kbd_mask = _div(_iota((2 * c, PAIR_K), 0), c) == _div(_iota((2 * c, PAIR_K), 1), DK_A)
    ibd_mask = _div(_iota((2 * c, PAIR_V), 0), c) == _div(_iota((2 * c, PAIR_V), 1), DV_A)
    sbd_mask = _div(_iota((PAIR_K, PAIR_V), 0), DK_A) == _div(_iota((PAIR_K, PAIR_V), 1), DV_A)
    causal = _mod(_iota((c, 2 * c), 1), c) <= _iota((c, 2 * c), 0)
    r16 = _iota((16, PAIR_K), 0)
    decay_rhs = jnp.concatenate([jnp.zeros((16, PAIR_V), F32), jnp.ones((16, PAIR_V), F32)], axis=1)

    for p in range(N_PAIR):
        q = proj(OFF_Q + p * PAIR_K, PAIR_K)
        fl = proj(OFF_F + p * PAIR_K, PAIR_K)
        iv = proj(OFF_I + p * PAIR_V, PAIR_V)
        qf, logf, kk = _hgrn_features(q, fl, lb[:, p * PAIR_K:(p + 1) * PAIR_K])
        state = s_ref[p]
        for ci in range(t_tile // c):
            rows = slice(ci * c, (ci + 1) * c)
            lf = logf[rows]
            cum = _dot_split(cum_m, lf)
            a = cum[:c]
            b_last = cum[c:c + 1]
            b_mid = lf[0:1] - a[0:1]
            qt = qf[rows] * jnp.exp(a)
            kt = kk[rows] * jnp.exp(-a)
            qi = qt * jnp.exp(b_mid)
            kh = kt * jnp.exp(b_last - b_mid)
            d = jnp.exp(b_last)
            ic = iv[rows]
            kbd = jnp.where(kbd_mask, jnp.concatenate([kt, kt], axis=0), 0.0)
            sc = jnp.where(causal, _dot_nt(qt, kbd), 0.0)
            ibd = jnp.where(ibd_mask, jnp.concatenate([ic, ic], axis=0), 0.0)
            o_ref[rows, p * PAIR_V:(p + 1) * PAIR_V] = _dot(sc, ibd) + _dot(qi, state)
            d_hi = d.astype(BF16).astype(F32)
            d_rows = jnp.where(r16 == 0, d_hi, jnp.where(r16 == 1, d - d_hi, 0.0))
            upd_l = jnp.concatenate([kh, d_rows], axis=0)
            upd_r = jnp.concatenate(
                [jnp.concatenate([ic, jnp.zeros((c, PAIR_V), F32)], axis=1), decay_rhs], axis=0)
            sd = _dot_tn(upd_l, upd_r)
            state = jnp.where(sbd_mask, sd[:, PAIR_V:] * state + sd[:, :PAIR_V], 0.0)
        s_ref[p] = state

    gu = jax.nn.gelu(proj(OFF_U, D_B))
    vn = _layer_norm(jax.nn.gelu(proj(OFF_V, D_B)), lng_ref[...], lnb_ref[...])
    tril = _iota((GMLP_CHUNK, GMLP_CHUNK), 1) <= _iota((GMLP_CHUNK, GMLP_CHUNK), 0)
    for g in range(H_B):
        w = jnp.where(tril, ws_ref[g], 0.0)
        cols = slice(g * CH_B, (g + 1) * CH_B)
        for n in range(t_tile // GMLP_CHUNK):
            rows = slice(n * GMLP_CHUNK, (n + 1) * GMLP_CHUNK)
            ob_ref[rows, cols] = gu[rows, cols] * (_dot(w, vn[rows, cols]) + bsb_ref[g])

    hb = jax.nn.sigmoid(proj(OFF_GB, D_MODEL)) * _dot(ob_ref[...], wpb_ref[...])
    y_ref[0] = _mix_out(x, o_ref[...], proj(OFF_OG, D_VA), jax.nn.sigmoid(proj(OFF_GA, D_MODEL)), hb,
                        hng_ref[...], wpa_ref, wo_ref, gpost_ref[...])

    @pl.when(j == pl.num_programs(1) - 1)
    def _():
        for h in range(H_A):
            hh = h % 2
            sp_ref[0, h] = s_ref[h // 2, hh * DK_A:(hh + 1) * DK_A, hh * DV_A:(hh + 1) * DV_A]


def _const_spec(shape):
    return pl.BlockSpec(shape, lambda *_: (0,) * len(shape), pipeline_mode=pl.Buffered(1))


def _prompt_mixer(x, lbp, gpre, win, hng, lng, lnb, ws, bsb, wpa, wpb, wo, gpost):
    nb, seq, _ = x.shape
    grid = (nb, seq // T_MIX)
    weights = (lbp, gpre, win, hng, lng, lnb, ws, bsb, wpa, wpb, wo, gpost)
    return pl.pallas_call(
        _prompt_mixer_kernel,
        grid=grid,
        in_specs=[pl.BlockSpec((1, T_MIX, D_MODEL), lambda b, j: (b, j, 0))]
        + [_const_spec(w.shape) for w in weights],
        out_specs=(pl.BlockSpec((1, T_MIX, D_MODEL), lambda b, j: (b, j, 0)),
                   pl.BlockSpec((1, H_A, DK_A, DV_A), lambda b, j: (b, 0, 0, 0))),
        out_shape=(jax.ShapeDtypeStruct(x.shape, F32),
                   jax.ShapeDtypeStruct((nb, H_A, DK_A, DV_A), F32)),
        scratch_shapes=[pltpu.VMEM((N_PAIR, PAIR_K, PAIR_V), F32),
                        pltpu.VMEM((T_MIX, D_VA), F32),
                        pltpu.VMEM((T_MIX, D_B), F32)],
        compiler_params=pltpu.CompilerParams(dimension_semantics=("arbitrary", "arbitrary"),
                                             vmem_limit_bytes=VMEM_LIMIT),
        name="prompt_mixer",
    )(x, *weights)


def _prompt_ffn_kernel(x_ref, gpre_ref, wup_ref, cw_ref, cb_ref, wdn_ref, gpost_ref, y_ref, cp_ref,
                       tail_ref, a_ref):
    j = pl.program_id(1)
    t_tile = x_ref.shape[1]

    @pl.when(j == 0)
    def _():
        tail_ref[...] = jnp.zeros_like(tail_ref)

    x = x_ref[0]
    xn = _rms(x, gpre_ref[...]).astype(BF16)
    row = _iota((t_tile, FFN_CB), 0)

    def conv(c0):
        cols = slice(c0, c0 + FFN_CB)
        up = jnp.dot(xn, wup_ref[:, cols], preferred_element_type=F32)
        p0 = tail_ref[6:7, cols]
        p1 = tail_ref[7:8, cols]
        tail_ref[:, cols] = up[t_tile - 8:]
        m1 = jnp.where(row == 0, p1, pltpu.roll(up, 1, 0))
        m2 = jnp.where(row == 0, p0, jnp.where(row == 1, p1, pltpu.roll(up, 2, 0)))
        return cb_ref[:, cols] + cw_ref[0:1, cols] * m2 + cw_ref[1:2, cols] * m1 + cw_ref[2:3, cols] * up

    for blk in range(D_FF // FFN_CB):
        c0 = blk * FFN_CB
        a_ref[:, c0:c0 + FFN_CB] = (jax.nn.gelu(conv(c0)) * conv(D_FF + c0)).astype(BF16)

    y_ref[0] = x + _rms(jnp.dot(a_ref[...], wdn_ref[...], preferred_element_type=F32), gpost_ref[...])

    @pl.when(j == pl.num_programs(1) - 1)
    def _():
        cp_ref[0] = tail_ref[6:8, :]


def _prompt_ffn(x, gpre, wup, cw, cb, wdn, gpost):
    nb, seq, _ = x.shape
    weights = (gpre, wup, cw, cb, wdn, gpost)
    return pl.pallas_call(
        _prompt_ffn_kernel,
        grid=(nb, seq // T_FFN),
        in_specs=[pl.BlockSpec((1, T_FFN, D_MODEL), lambda b, j: (b, j, 0))]
        + [_const_spec(w.shape) for w in weights],
        out_specs=(pl.BlockSpec((1, T_FFN, D_MODEL), lambda b, j: (b, j, 0)),
                   pl.BlockSpec((1, CONV_W - 1, 2 * D_FF), lambda b, j: (b, 0, 0))),
        out_shape=(jax.ShapeDtypeStruct(x.shape, F32),
                   jax.ShapeDtypeStruct((nb, CONV_W - 1, 2 * D_FF), F32)),
        scratch_shapes=[pltpu.VMEM((8, 2 * D_FF), F32),
                        pltpu.VMEM((T_FFN, D_FF), BF16)],
        compiler_params=pltpu.CompilerParams(dimension_semantics=("arbitrary", "arbitrary"),
                                             vmem_limit_bytes=VMEM_LIMIT),
        name="prompt_ffn",
    )(x, *weights)


def _sample_pre_kernel(wsm_ref, bsm_ref, x_ref, lbp_ref, gpre_ref, win_ref, lng_ref, lnb_ref, wpb_ref,
                       qi_ref, kh_ref, d_ref, i_ref, oin_ref, og_ref, gas_ref, hb_ref, vn_ref):
    n_rows = x_ref.shape[0]
    seq = 4
    x = x_ref[...]
    xn = _rms(x, gpre_ref[...]).astype(BF16)
    lb = _lower_bound(lbp_ref[...])

    def proj(off, width):
        return jnp.dot(xn, win_ref[:, off:off + width], preferred_element_type=F32)

    def tok(width):
        return _mod(_iota((n_rows, width), 0), seq)

    def shift(v, k):
        return pltpu.roll(v, k, 0) if k else v

    qf, logf, kk = _hgrn_features(proj(OFF_Q, D_QA), proj(OFF_F, D_QA), lb)
    iv = proj(OFF_I, D_VA)
    tq = tok(D_QA)
    b = logf + jnp.where(tq >= 1, shift(logf, 1), 0.0)
    b = b + jnp.where(tq >= 2, shift(b, 2), 0.0)
    b_last = jnp.where(tq == seq - 1, b, 0.0)
    b_last = b_last + pltpu.roll(b_last, n_rows - 1, 0)
    b_last = b_last + pltpu.roll(b_last, n_rows - 2, 0)
    qi_ref[...] = qf * jnp.exp(b)
    kh_ref[...] = kk * jnp.exp(b_last - b)
    d_ref[...] = jnp.exp(b_last)
    i_ref[...] = iv

    head_sum = _ones_where(_div(_iota((D_QA, D_VA), 0), DK_A) == _div(_iota((D_QA, D_VA), 1), DV_A))
    tv = tok(D_VA)
    o = jnp.zeros((n_rows, D_VA), F32)
    for k in range(seq):
        pair = qf * shift(kk, k) * jnp.exp(b - shift(b, k))
        sc = jnp.dot(jnp.where(tq >= k, pair, 0.0).astype(BF16), head_sum, preferred_element_type=F32)
        o = o + jnp.where(tv >= k, sc * shift(iv, k), 0.0)
    oin_ref[...] = o

    gu = jax.nn.gelu(proj(OFF_U, D_B))
    vn = _layer_norm(jax.nn.gelu(proj(OFF_V, D_B)), lng_ref[...], lnb_ref[...])
    vn_ref[...] = vn
    tg = tok(CH_B)
    obs = []
    for g in range(H_B):
        vg = vn[:, g * CH_B:(g + 1) * CH_B]
        s = jnp.zeros((n_rows, CH_B), F32)
        for t in range(seq):
            s = s + jnp.where(tg == t, bsm_ref[g * seq + t], 0.0)
        for k in range(seq):
            coef = jnp.zeros((n_rows, CH_B), F32)
            for t in range(k, seq):
                coef = coef + jnp.where(tg == t, wsm_ref[(g * seq + t) * seq + (t - k)], 0.0)
            s = s + coef * shift(vg, k)
        obs.append(gu[:, g * CH_B:(g + 1) * CH_B] * s)
    ob = jnp.concatenate(obs, axis=1)
    hb_ref[...] = jax.nn.sigmoid(proj(OFF_GB, D_MODEL)) * _dot(ob, wpb_ref[...])
    gas_ref[...] = jax.nn.sigmoid(proj(OFF_GA, D_MODEL))
    og_ref[...] = proj(OFF_OG, D_VA)


def _sample_pre(wsm, bsm, x, lbp, gpre, win, lng, lnb, wpb):
    n_rows = x.shape[0]
    shapes = [(n_rows, D_QA), (n_rows, D_QA), (n_rows, D_QA), (n_rows, D_VA), (n_rows, D_VA),
              (n_rows, D_VA), (n_rows, D_MODEL), (n_rows, D_MODEL), (n_rows, D_B)]
    smem = pl.BlockSpec(memory_space=pltpu.SMEM)
    return pl.pallas_call(
        _sample_pre_kernel,
        in_specs=[smem, smem] + [pl.BlockSpec(memory_space=pltpu.VMEM)] * 7,
        out_shape=tuple(jax.ShapeDtypeStruct(s, F32) for s in shapes),
        compiler_params=pltpu.CompilerParams(vmem_limit_bytes=VMEM_LIMIT),
        name="sample_pre",
    )(wsm, bsm, x, lbp, gpre, win, lng, lnb, wpb)


def _sample_state_kernel(qi_ref, kh_ref, d_ref, i_ref, s_ref, o_ref, sn_ref):
    seq = 4
    blk = 16
    rows_seq = _div(_iota((blk, DV_A), 0), seq)
    for m in range(qi_ref.shape[0] // blk):
        rows = slice(m * blk, (m + 1) * blk)
        outs = []
        for h in range(H_A):
            kcols = slice(h * DK_A, (h + 1) * DK_A)
            vcols = slice(h * DV_A, (h + 1) * DV_A)
            q16 = qi_ref[rows, kcols].astype(BF16)
            kt = kh_ref[rows, kcols].T
            dt = d_ref[rows, kcols].T
            i16 = i_ref[rows, vcols]
            oh = jnp.zeros((blk, DV_A), F32)
            for bb in range(blk // seq):
                idx = (m * (blk // seq) + bb) * H_A + h
                s0 = s_ref[idx]
                r = jnp.dot(q16, s0.astype(BF16), preferred_element_type=F32)
                oh = jnp.where(rows_seq == bb, r, oh)
                sadd = _dot(kt, jnp.where(rows_seq == bb, i16, 0.0))
                sn_ref[idx] = dt[:, bb * seq:bb * seq + 1] * s0 + sadd
            outs.append(oh)
        o_ref[rows, :] = jnp.concatenate(outs, axis=1)


def _sample_state(qi, kh, d, iv, state):
    n_rows = qi.shape[0]
    rb = SAMPLE_BB * 4
    sb = SAMPLE_BB * H_A
    return pl.pallas_call(
        _sample_state_kernel,
        grid=(n_rows // rb,),
        in_specs=[pl.BlockSpec((rb, D_QA), lambda i: (i, 0)),
                  pl.BlockSpec((rb, D_QA), lambda i: (i, 0)),
                  pl.BlockSpec((rb, D_QA), lambda i: (i, 0)),
                  pl.BlockSpec((rb, D_VA), lambda i: (i, 0)),
                  pl.BlockSpec((sb, DK_A, DV_A), lambda i: (i, 0, 0))],
        out_specs=(pl.BlockSpec((rb, D_VA), lambda i: (i, 0)),
                   pl.BlockSpec((sb, DK_A, DV_A), lambda i: (i, 0, 0))),
        out_shape=(jax.ShapeDtypeStruct((n_rows, D_VA), F32),
                   jax.ShapeDtypeStruct(state.shape, F32)),
        compiler_params=pltpu.CompilerParams(dimension_semantics=("arbitrary",),
                                             vmem_limit_bytes=VMEM_LIMIT),
        name="sample_state",
    )(qi, kh, d, iv, state)


def _sample_post_kernel(x_ref, oin_ref, oint_ref, og_ref, gas_ref, hb_ref, hng_ref, wpa_ref, wo_ref,
                        gpost_ref, y_ref):
    y_ref[...] = _mix_out(x_ref[...], oin_ref[...] + oint_ref[...], og_ref[...], gas_ref[...], hb_ref[...],
                          hng_ref[...], wpa_ref, wo_ref, gpost_ref[...])


def _sample_post(x, oin, oint, og, gas, hb, hng, wpa, wo, gpost):
    return pl.pallas_call(
        _sample_post_kernel,
        out_shape=jax.ShapeDtypeStruct(x.shape, F32),
        compiler_params=pltpu.CompilerParams(vmem_limit_bytes=VMEM_LIMIT),
        name="sample_post",
    )(x, oin, oint, og, gas, hb, hng, wpa, wo, gpost)


def _sample_ffn_kernel(x_ref, gpre_ref, wug_ref, wuv_ref, cg_ref, cv_ref, cwg_ref, cwv_ref, cbg_ref, cbv_ref,
                       wdn_ref, gpost_ref, y_ref, csg_ref, csv_ref, xn_ref, acc_ref):
    blk = pl.program_id(0)
    n_rows = x_ref.shape[0]
    n_cache = cg_ref.shape[0]
    seq = 4

    @pl.when(blk == 0)
    def _():
        xn_ref[...] = _rms(x_ref[...], gpre_ref[...]).astype(BF16)
        acc_ref[...] = jnp.zeros_like(acc_ref)

    xn = xn_ref[...]
    tok = _mod(_iota((n_rows, FFN_CB), 0), seq)
    er, ec = _iota((2 * n_rows, n_cache), 0), _iota((2 * n_rows, n_cache), 1)
    r_seq, r_tok, c_seq, c_j = _div(_mod(er, n_rows), seq), _mod(er, seq), _div(ec, 2), _mod(ec, 2)
    src_j = jnp.where(er < n_rows, r_tok, jnp.where(r_tok == 0, 1, -1))
    expand = _ones_where((r_seq == c_seq) & (src_j == c_j))
    pr, pc = _iota((n_cache, n_rows), 0), _iota((n_cache, n_rows), 1)
    pick = _ones_where((_div(pr, 2) == _div(pc, seq)) & (_mod(pc, seq) == _mod(pr, 2) + 2))

    def conv(wu_ref, cache_ref, cw_ref, cb_ref, cs_ref):
        up = jnp.dot(xn, wu_ref[...], preferred_element_type=F32)
        cs_ref[...] = _dot_split(pick, up)
        old = _dot_split(expand, cache_ref[...])
        m2 = jnp.where(tok >= 2, pltpu.roll(up, 2, 0), old[:n_rows])
        m1 = jnp.where(tok >= 1, pltpu.roll(up, 1, 0), old[n_rows:])
        return cb_ref[...] + cw_ref[0:1, :] * m2 + cw_ref[1:2, :] * m1 + cw_ref[2:3, :] * up

    a = jax.nn.gelu(conv(wug_ref, cg_ref, cwg_ref, cbg_ref, csg_ref)) * conv(wuv_ref, cv_ref, cwv_ref, cbv_ref, csv_ref)
    acc_ref[...] += _dot(a, wdn_ref[...])

    @pl.when(blk == pl.num_programs(0) - 1)
    def _():
        y_ref[...] = x_ref[...] + _rms(acc_ref[...], gpost_ref[...])


def _sample_ffn(x, gpre, wup, cache, cw, cb, wdn, gpost):
    n_rows = x.shape[0]
    n_cache = cache.shape[0]
    nblk = D_FF // FFN_CB
    gate = lambda i: (0, i)
    val = lambda i: (0, i + nblk)
    full = lambda shape: pl.BlockSpec(shape, lambda i: (0,) * len(shape))
    return pl.pallas_call(
        _sample_ffn_kernel,
        grid=(nblk,),
        in_specs=[full((n_rows, D_MODEL)), full((1, D_MODEL)),
                  pl.BlockSpec((D_MODEL, FFN_CB), gate), pl.BlockSpec((D_MODEL, FFN_CB), val),
                  pl.BlockSpec((n_cache, FFN_CB), gate), pl.BlockSpec((n_cache, FFN_CB), val),
                  pl.BlockSpec((CONV_W, FFN_CB), gate), pl.BlockSpec((CONV_W, FFN_CB), val),
                  pl.BlockSpec((1, FFN_CB), gate), pl.BlockSpec((1, FFN_CB), val),
                  pl.BlockSpec((FFN_CB, D_MODEL), lambda i: (i, 0)), full((1, D_MODEL))],
        out_specs=(full((n_rows, D_MODEL)),
                   pl.BlockSpec((n_cache, FFN_CB), gate), pl.BlockSpec((n_cache, FFN_CB), gate)),
        out_shape=(jax.ShapeDtypeStruct((n_rows, D_MODEL), F32),
                   jax.ShapeDtypeStruct((n_cache, D_FF), F32),
                   jax.ShapeDtypeStruct((n_cache, D_FF), F32)),
        scratch_shapes=[pltpu.VMEM((n_rows, D_MODEL), BF16), pltpu.VMEM((n_rows, D_MODEL), F32)],
        compiler_params=pltpu.CompilerParams(dimension_semantics=("arbitrary",),
                                             vmem_limit_bytes=VMEM_LIMIT),
        name="sample_ffn",
    )(x, gpre, wup, wup, cache, cache, cw, cw, cb, cb, wdn, gpost)


def kernel(x_prompt, x_sample, state_hgrn, cache_ffn_conv, lb_param, mix_pre_g, w_in, hgrn_norm_g, gmlp_ln_g, gmlp_ln_b, w_s, b_s, w_pa, w_pb, w_o, mix_post_g, ffn_pre_g, w_up, conv_w, conv_b, w_down, ffn_post_g):
    nb_s, seq_s, _ = x_sample.shape
    row = lambda v: v.reshape(1, -1)
    win, wpa, wpb, wo = (w[0].astype(BF16) for w in (w_in, w_pa, w_pb, w_o))
    wup, wdn, ws = w_up[0].astype(BF16), w_down[0].astype(BF16), w_s[0].astype(BF16)
    gpre, gpost, fpre, fpost = row(mix_pre_g[0]), row(mix_post_g[0]), row(ffn_pre_g[0]), row(ffn_post_g[0])
    lng, lnb, cb = row(gmlp_ln_g[0]), row(gmlp_ln_b[0]), row(conv_b[0])
    hng = row(jnp.tile(hgrn_norm_g[0], H_A))
    bsb = jnp.broadcast_to(b_s[0][:, :, None], (H_B, GMLP_CHUNK, CH_B))

    x1, sp = _prompt_mixer(x_prompt, lb_param, gpre, win, hng, lng, lnb, ws, bsb, wpa, wpb, wo, gpost)
    yp, cp = _prompt_ffn(x1, fpre, wup, conv_w[0], cb, wdn, fpost)

    xs = x_sample.reshape(nb_s * seq_s, D_MODEL)
    wsm = w_s[0][:, :seq_s, :seq_s].reshape(-1)
    bsm = b_s[0][:, :seq_s].reshape(-1)
    qi, kh, d, iv, oin, og, gas, hb, vn = _sample_pre(wsm, bsm, xs, lb_param, gpre, win, lng, lnb, wpb)
    oint, ss = _sample_state(qi, kh, d, iv, state_hgrn[0].reshape(nb_s * H_A, DK_A, DV_A))
    xs1 = _sample_post(xs, oin, oint, og, gas, hb, hng, wpa, wo, gpost)
    cache = cache_ffn_conv[0].reshape(nb_s * (CONV_W - 1), 2 * D_FF)
    ys, csg, csv = _sample_ffn(xs1, fpre, wup, cache, conv_w[0], cb, wdn, fpost)
    cs = jnp.concatenate([csg, csv], axis=1).reshape(nb_s, CONV_W - 1, 2 * D_FF)

    return (yp, ys.reshape(nb_s, seq_s, D_MODEL), sp[None],
            ss.reshape(1, nb_s, H_A, DK_A, DV_A), cp[None], cs[None],
            vn.reshape(1, nb_s, seq_s, D_B))
```

```python
import functools

import jax
import jax.numpy as jnp
from jax import lax
from jax.experimental import pallas as pl
from jax.experimental.pallas import tpu as pltpu

F32 = jnp.float32
BF16 = jnp.bfloat16

D_MODEL = 1024
H_A, DK_A, DV_A = 8, 128, 64
D_QA, D_VA = H_A * DK_A, H_A * DV_A
H_B, CH_B = 4, 128
D_B = H_B * CH_B
GMLP_CHUNK = 128
D_FF = 2816
CONV_W = 3
EPS = 1e-6

OFF_Q, OFF_F, OFF_I, OFF_OG, OFF_U, OFF_V, OFF_GA, OFF_GB = 0, 1024, 2048, 2560, 3072, 3584, 4096, 5120
D_IN = 6144

HGRN_C = 64
PAIR_K = 2 * DK_A
PAIR_V = 2 * DV_A
N_PAIR = H_A // 2
T_MIX = 256
T_FFN = 512
FFN_CB = 256
SAMPLE_BB = 8
VMEM_LIMIT = 56 * 1024 * 1024


def _dot(a, b):
    return jnp.dot(a.astype(BF16), b.astype(BF16), preferred_element_type=F32)


def _dot_nt(a, b):
    return lax.dot_general(a.astype(BF16), b.astype(BF16), (((1,), (1,)), ((), ())),
                           preferred_element_type=F32)


def _dot_tn(a, b):
    return lax.dot_general(a.astype(BF16), b.astype(BF16), (((0,), (0,)), ((), ())),
                           preferred_element_type=F32)


def _dot_split(m, x):
    hi = x.astype(BF16)
    lo = (x - hi.astype(F32)).astype(BF16)
    m = m.astype(BF16)
    return (jnp.dot(m, hi, preferred_element_type=F32) + jnp.dot(m, lo, preferred_element_type=F32))


def _rms(x, g):
    return x * lax.rsqrt(jnp.mean(x * x, axis=-1, keepdims=True) + EPS) * g


def _layer_norm(x, g, b):
    xc = x - jnp.mean(x, axis=-1, keepdims=True)
    return xc * lax.rsqrt(jnp.mean(xc * xc, axis=-1, keepdims=True) + EPS) * g + b


def _lower_bound(lbp):
    e = jnp.exp(lbp - jnp.max(lbp, axis=0, keepdims=True))
    return e[0:1] / jnp.sum(e, axis=0, keepdims=True)


def _hgrn_features(q, f_logit, lb):
    f = lb + (1.0 - lb) * jax.nn.sigmoid(f_logit)
    return q * jax.nn.sigmoid(q), jnp.log(f), 1.0 - f


def _iota(shape, dim):
    return lax.broadcasted_iota(jnp.int32, shape, dim)


def _div(x, n):
    assert n & (n - 1) == 0
    return lax.shift_right_logical(x, n.bit_length() - 1)


def _mod(x, n):
    assert n & (n - 1) == 0
    return x & (n - 1)


def _ones_where(cond):
    return jnp.where(cond, 1.0, 0.0).astype(BF16)


def _head_mean_square(o):
    same_head = _ones_where(_div(_iota((D_VA, D_VA), 0), DV_A) == _div(_iota((D_VA, D_VA), 1), DV_A))
    return _dot_split_rhs(o * o, same_head) * (1.0 / DV_A)


def _dot_split_rhs(x, m):
    hi = x.astype(BF16)
    lo = (x - hi.astype(F32)).astype(BF16)
    return (jnp.dot(hi, m, preferred_element_type=F32) + jnp.dot(lo, m, preferred_element_type=F32))


def _mix_out(x, o, og, ga_sig, hb, hng, wpa_ref, wo_ref, gpost):
    oa = o * lax.rsqrt(_head_mean_square(o) + EPS) * hng * (og * jax.nn.sigmoid(og))
    h = ga_sig * _dot(oa, wpa_ref[...]) + hb
    return x + _rms(_dot(h, wo_ref[...]), gpost)


def _prompt_mixer_kernel(x_ref, lbp_ref, gpre_ref, win_ref, hng_ref, lng_ref, lnb_ref, ws_ref, bsb_ref,
                         wpa_ref, wpb_ref, wo_ref, gpost_ref, y_ref, sp_ref,
                         st_ref, qt_ref, kt_ref, qi_ref, kh_ref, iv_ref, d_ref, sc_ref, upd_ref, stb_ref,
                         o_ref, ob_ref, tr_ref):
    j = pl.program_id(1)
    t_tile = x_ref.shape[1]
    c = HGRN_C
    n_chunk = t_tile // c

    @pl.when(j == 0)
    def _():
        st_ref[...] = jnp.zeros_like(st_ref)

    x = x_ref[0]
    xn = _rms(x, gpre_ref[...]).astype(BF16)
    lb = _lower_bound(lbp_ref[...])

    def proj(off, width):
        return jnp.dot(xn, win_ref[:, off:off + width], preferred_element_type=F32)

    qf, logf, kk = _hgrn_features(proj(OFF_Q, D_QA), proj(OFF_F, D_QA), lb)
    iv_ref[...] = proj(OFF_I, D_VA).astype(BF16)
    r, s = _iota((c + 8, c), 0), _iota((c + 8, c), 1)
    cum_m = jnp.where(r < c, jnp.where(s <= r, 1.0, 0.0) - jnp.where(s < c // 2, 1.0, 0.0), 1.0)
    for ci in range(n_chunk):
        rows = slice(ci * c, (ci + 1) * c)
        lf = logf[rows]
        cum = _dot_split(cum_m, lf)
        a = cum[:c]
        b_last = cum[c:c + 1]
        b_mid = lf[0:1] - a[0:1]
        qt = qf[rows] * jnp.exp(a)
        kt = kk[rows] * jnp.exp(-a)
        qt_ref[rows, :] = qt.astype(BF16)
        kt_ref[rows, :] = kt.astype(BF16)
        qi_ref[rows, :] = (qt * jnp.exp(b_mid)).astype(BF16)
        kh_ref[rows, :] = (kt * jnp.exp(b_last - b_mid)).astype(BF16)
        d_ref[ci:ci + 1, :] = jnp.exp(b_last)

    kbd_mask = _div(_iota((2 * c, PAIR_K), 0), c) == _div(_iota((2 * c, PAIR_K), 1), DK_A)
    ibd_mask = _div(_iota((2 * c, PAIR_V), 0), c) == _div(_iota((2 * c, PAIR_V), 1), DV_A)
    sbd_mask = _div(_iota((PAIR_V, PAIR_K), 0), DV_A) == _div(_iota((PAIR_V, PAIR_K), 1), DK_A)
    causal = _mod(_iota((c, 2 * c), 1), c) <= _iota((c, 2 * c), 0)
    zero = jnp.zeros((), BF16)

    for p in range(N_PAIR):
        kcols = slice(p * PAIR_K, (p + 1) * PAIR_K)
        vcols = slice(p * PAIR_V, (p + 1) * PAIR_V)
        for ci in range(n_chunk):
            rows = slice(ci * c, (ci + 1) * c)
            kt = kt_ref[rows, kcols]
            kbd = jnp.where(kbd_mask, jnp.concatenate([kt, kt], axis=0), zero)
            sc = lax.dot_general(qt_ref[rows, kcols], kbd, (((1,), (1,)), ((), ())), preferred_element_type=F32)
            sc_ref[rows, vcols] = jnp.where(causal, sc, 0.0).astype(BF16)
            inc = lax.dot_general(iv_ref[rows, vcols], kh_ref[rows, kcols], (((0,), (0,)), ((), ())),
                                  preferred_element_type=F32)
            upd_ref[p * n_chunk + ci] = jnp.where(sbd_mask, inc, 0.0)

    for p in range(N_PAIR):
        kcols = slice(p * PAIR_K, (p + 1) * PAIR_K)
        st = st_ref[p]
        for ci in range(n_chunk):
            stb_ref[p * n_chunk + ci] = st.astype(BF16)
            st = st * d_ref[ci:ci + 1, kcols] + upd_ref[p * n_chunk + ci]
        st_ref[p] = st

    for p in range(N_PAIR):
        kcols = slice(p * PAIR_K, (p + 1) * PAIR_K)
        vcols = slice(p * PAIR_V, (p + 1) * PAIR_V)
        for ci in range(n_chunk):
            rows = slice(ci * c, (ci + 1) * c)
            ic = iv_ref[rows, vcols]
            ibd = jnp.where(ibd_mask, jnp.concatenate([ic, ic], axis=0), zero)
            o_ref[rows, vcols] = (
                jnp.dot(sc_ref[rows, vcols], ibd, preferred_element_type=F32)
                + lax.dot_general(qi_ref[rows, kcols], stb_ref[p * n_chunk + ci], (((1,), (1,)), ((), ())),
                                  preferred_element_type=F32))

    gu = jax.nn.gelu(proj(OFF_U, D_B))
    vn = _layer_norm(jax.nn.gelu(proj(OFF_V, D_B)), lng_ref[...], lnb_ref[...])
    tril = _iota((GMLP_CHUNK, GMLP_CHUNK), 1) <= _iota((GMLP_CHUNK, GMLP_CHUNK), 0)
    for g in range(H_B):
        w = jnp.where(tril, ws_ref[g], zero)
        cols = slice(g * CH_B, (g + 1) * CH_B)
        for n in range(t_tile // GMLP_CHUNK):
            rows = slice(n * GMLP_CHUNK, (n + 1) * GMLP_CHUNK)
            ob_ref[rows, cols] = gu[rows, cols] * (_dot(w, vn[rows, cols]) + bsb_ref[g])

    hb = jax.nn.sigmoid(proj(OFF_GB, D_MODEL)) * _dot(ob_ref[...], wpb_ref[...])
    y_ref[0] = _mix_out(x, o_ref[...], proj(OFF_OG, D_VA), jax.nn.sigmoid(proj(OFF_GA, D_MODEL)), hb,
                        hng_ref[...], wpa_ref, wo_ref, gpost_ref[...])

    @pl.when(j == pl.num_programs(1) - 1)
    def _():
        for p in range(N_PAIR):
            tr_ref[...] = st_ref[p].T
            for hh in range(2):
                sp_ref[0, 2 * p + hh] = tr_ref[hh * DK_A:(hh + 1) * DK_A, hh * DV_A:(hh + 1) * DV_A]


def _const_spec(shape):
    return pl.BlockSpec(shape, lambda *_: (0,) * len(shape), pipeline_mode=pl.Buffered(1))


def _prompt_mixer(x, lbp, gpre, win, hng, lng, lnb, ws, bsb, wpa, wpb, wo, gpost):
    nb, seq, _ = x.shape
    grid = (nb, seq // T_MIX)
    n_pc = N_PAIR * (T_MIX // HGRN_C)
    assert T_MIX // HGRN_C <= 8
    weights = (lbp, gpre, win, hng, lng, lnb, ws, bsb, wpa, wpb, wo, gpost)
    return pl.pallas_call(
        _prompt_mixer_kernel,
        grid=grid,
        in_specs=[pl.BlockSpec((1, T_MIX, D_MODEL), lambda b, j: (b, j, 0))]
        + [_const_spec(w.shape) for w in weights],
        out_specs=(pl.BlockSpec((1, T_MIX, D_MODEL), lambda b, j: (b, j, 0)),
                   pl.BlockSpec((1, H_A, DK_A, DV_A), lambda b, j: (b, 0, 0, 0))),
        out_shape=(jax.ShapeDtypeStruct(x.shape, F32),
                   jax.ShapeDtypeStruct((nb, H_A, DK_A, DV_A), F32)),
        scratch_shapes=[pltpu.VMEM((N_PAIR, PAIR_V, PAIR_K), F32),
                        pltpu.VMEM((T_MIX, D_QA), BF16),
                        pltpu.VMEM((T_MIX, D_QA), BF16),
                        pltpu.VMEM((T_MIX, D_QA), BF16),
                        pltpu.VMEM((T_MIX, D_QA), BF16),
                        pltpu.VMEM((T_MIX, D_VA), BF16),
                        pltpu.VMEM((8, D_QA), F32),
                        pltpu.VMEM((T_MIX, D_VA), BF16),
                        pltpu.VMEM((n_pc, PAIR_V, PAIR_K), F32),
                        pltpu.VMEM((n_pc, PAIR_V, PAIR_K), BF16),
                        pltpu.VMEM((T_MIX, D_VA), F32),
                        pltpu.VMEM((T_MIX, D_B), F32),
                        pltpu.VMEM((PAIR_K, PAIR_V), F32)],
        compiler_params=pltpu.CompilerParams(dimension_semantics=("arbitrary", "arbitrary"),
                                             vmem_limit_bytes=VMEM_LIMIT),
        name="prompt_mixer",
    )(x, *weights)


def _prompt_ffn_kernel(x_ref, gpre_ref, wup_ref, cw_ref, cb_ref, wdn_ref, gpost_ref, y_ref, cp_ref,
                       tail_ref, a_ref):
    j = pl.program_id(1)
    t_tile = x_ref.shape[1]

    @pl.when(j == 0)
    def _():
        tail_ref[...] = jnp.zeros_like(tail_ref)

    x = x_ref[0]
    xn = _rms(x, gpre_ref[...]).astype(BF16)
    row = _iota((t_tile, FFN_CB), 0)

    def conv(c0):
        cols = slice(c0, c0 + FFN_CB)
        up = jnp.dot(xn, wup_ref[:, cols], preferred_element_type=F32)
        p0 = tail_ref[6:7, cols]
        p1 = tail_ref[7:8, cols]
        tail_ref[:, cols] = up[t_tile - 8:]
        m1 = jnp.where(row == 0, p1, pltpu.roll(up, 1, 0))
        m2 = jnp.where(row == 0, p0, jnp.where(row == 1, p1, pltpu.roll(up, 2, 0)))
        return cb_ref[:, cols] + cw_ref[0:1, cols] * m2 + cw_ref[1:2, cols] * m1 + cw_ref[2:3, cols] * up

    for blk in range(D_FF // FFN_CB):
        c0 = blk * FFN_CB
        a_ref[:, c0:c0 + FFN_CB] = (jax.nn.gelu(conv(c0)) * conv(D_FF + c0)).astype(BF16)

    y_ref[0] = x + _rms(jnp.dot(a_ref[...], wdn_ref[...], preferred_element_type=F32), gpost_ref[...])

    @pl.when(j == pl.num_programs(1) - 1)
    def _():
        cp_ref[0] = tail_ref[6:8, :]


def _prompt_ffn(x, gpre, wup, cw, cb, wdn, gpost):
    nb, seq, _ = x.shape
    weights = (gpre, wup, cw, cb, wdn, gpost)
    return pl.pallas_call(
        _prompt_ffn_kernel,
        grid=(nb, seq // T_FFN),
        in_specs=[pl.BlockSpec((1, T_FFN, D_MODEL), lambda b, j: (b, j, 0))]
        + [_const_spec(w.shape) for w in weights],
        out_specs=(pl.BlockSpec((1, T_FFN, D_MODEL), lambda b, j: (b, j, 0)),
                   pl.BlockSpec((1, CONV_W - 1, 2 * D_FF), lambda b, j: (b, 0, 0))),
        out_shape=(jax.ShapeDtypeStruct(x.shape, F32),
                   jax.ShapeDtypeStruct((nb, CONV_W - 1, 2 * D_FF), F32)),
        scratch_shapes=[pltpu.VMEM((8, 2 * D_FF), F32),
                        pltpu.VMEM((T_FFN, D_FF), BF16)],
        compiler_params=pltpu.CompilerParams(dimension_semantics=("arbitrary", "arbitrary"),
                                             vmem_limit_bytes=VMEM_LIMIT),
        name="prompt_ffn",
    )(x, *weights)


def _sample_pre_kernel(wsm_ref, bsm_ref, x_ref, lbp_ref, gpre_ref, win_ref, lng_ref, lnb_ref, wpb_ref,
                       qi_ref, kh_ref, d_ref, i_ref, oin_ref, og_ref, gas_ref, hb_ref, vn_ref):
    n_rows = x_ref.shape[0]
    seq = 4
    x = x_ref[...]
    xn = _rms(x, gpre_ref[...]).astype(BF16)
    lb = _lower_bound(lbp_ref[...])

    def proj(off, width):
        return jnp.dot(xn, win_ref[:, off:off + width], preferred_element_type=F32)

    def tok(width):
        return _mod(_iota((n_rows, width), 0), seq)

    def shift(v, k):
        return pltpu.roll(v, k, 0) if k else v

    qf, logf, kk = _hgrn_features(proj(OFF_Q, D_QA), proj(OFF_F, D_QA), lb)
    iv = proj(OFF_I, D_VA)
    tq = tok(D_QA)
    b = logf + jnp.where(tq >= 1, shift(logf, 1), 0.0)
    b = b + jnp.where(tq >= 2, shift(b, 2), 0.0)
    b_last = jnp.where(tq == seq - 1, b, 0.0)
    b_last = b_last + pltpu.roll(b_last, n_rows - 1, 0)
    b_last = b_last + pltpu.roll(b_last, n_rows - 2, 0)
    qi_ref[...] = qf * jnp.exp(b)
    kh_ref[...] = kk * jnp.exp(b_last - b)
    d_ref[...] = jnp.exp(b_last)
    i_ref[...] = iv

    head_sum = _ones_where(_div(_iota((D_QA, D_VA), 0), DK_A) == _div(_iota((D_QA, D_VA), 1), DV_A))
    tv = tok(D_VA)
    o = jnp.zeros((n_rows, D_VA), F32)
    for k in range(seq):
        pair = qf * shift(kk, k) * jnp.exp(b - shift(b, k))
        sc = jnp.dot(jnp.where(tq >= k, pair, 0.0).astype(BF16), head_sum, preferred_element_type=F32)
        o = o + jnp.where(tv >= k, sc * shift(iv, k), 0.0)
    oin_ref[...] = o

    gu = jax.nn.gelu(proj(OFF_U, D_B))
    vn = _layer_norm(jax.nn.gelu(proj(OFF_V, D_B)), lng_ref[...], lnb_ref[...])
    vn_ref[...] = vn
    tg = tok(CH_B)
    obs = []
    for g in range(H_B):
        vg = vn[:, g * CH_B:(g + 1) * CH_B]
        s = jnp.zeros((n_rows, CH_B), F32)
        for t in range(seq):
            s = s + jnp.where(tg == t, bsm_ref[g * seq + t], 0.0)
        for k in range(seq):
            coef = jnp.zeros((n_rows, CH_B), F32)
            for t in range(k, seq):
                coef = coef + jnp.where(tg == t, wsm_ref[(g * seq + t) * seq + (t - k)], 0.0)
            s = s + coef * shift(vg, k)
        obs.append(gu[:, g * CH_B:(g + 1) * CH_B] * s)
    ob = jnp.concatenate(obs, axis=1)
    hb_ref[...] = jax.nn.sigmoid(proj(OFF_GB, D_MODEL)) * _dot(ob, wpb_ref[...])
    gas_ref[...] = jax.nn.sigmoid(proj(OFF_GA, D_MODEL))
    og_ref[...] = proj(OFF_OG, D_VA)


def _sample_pre(wsm, bsm, x, lbp, gpre, win, lng, lnb, wpb):
    n_rows = x.shape[0]
    shapes = [(n_rows, D_QA), (n_rows, D_QA), (n_rows, D_QA), (n_rows, D_VA), (n_rows, D_VA),
              (n_rows, D_VA), (n_rows, D_MODEL), (n_rows, D_MODEL), (n_rows, D_B)]
    smem = pl.BlockSpec(memory_space=pltpu.SMEM)
    return pl.pallas_call(
        _sample_pre_kernel,
        in_specs=[smem, smem] + [pl.BlockSpec(memory_space=pltpu.VMEM)] * 7,
        out_shape=tuple(jax.ShapeDtypeStruct(s, F32) for s in shapes),
        compiler_params=pltpu.CompilerParams(vmem_limit_bytes=VMEM_LIMIT),
        name="sample_pre",
    )(wsm, bsm, x, lbp, gpre, win, lng, lnb, wpb)


def _sample_state_kernel(qi_ref, kh_ref, d_ref, i_ref, s_ref, o_ref, sn_ref):
    seq = 4
    blk = 16
    rows_seq = _div(_iota((blk, DV_A), 0), seq)
    for m in range(qi_ref.shape[0] // blk):
        rows = slice(m * blk, (m + 1) * blk)
        outs = []
        for h in range(H_A):
            kcols = slice(h * DK_A, (h + 1) * DK_A)
            vcols = slice(h * DV_A, (h + 1) * DV_A)
            q16 = qi_ref[rows, kcols].astype(BF16)
            kt = kh_ref[rows, kcols].T
            dt = d_ref[rows, kcols].T
            i16 = i_ref[rows, vcols]
            oh = jnp.zeros((blk, DV_A), F32)
            for bb in range(blk // seq):
                idx = (m * (blk // seq) + bb) * H_A + h
                s0 = s_ref[idx]
                r = jnp.dot(q16, s0.astype(BF16), preferred_element_type=F32)
                oh = jnp.where(rows_seq == bb, r, oh)
                sadd = _dot(kt, jnp.where(rows_seq == bb, i16, 0.0))
                sn_ref[idx] = dt[:, bb * seq:bb * seq + 1] * s0 + sadd
            outs.append(oh)
        o_ref[rows, :] = jnp.concatenate(outs, axis=1)


def _sample_state(qi, kh, d, iv, state):
    n_rows = qi.shape[0]
    rb = SAMPLE_BB * 4
    sb = SAMPLE_BB * H_A
    return pl.pallas_call(
        _sample_state_kernel,
        grid=(n_rows // rb,),
        in_specs=[pl.BlockSpec((rb, D_QA), lambda i: (i, 0)),
                  pl.BlockSpec((rb, D_QA), lambda i: (i, 0)),
                  pl.BlockSpec((rb, D_QA), lambda i: (i, 0)),
                  pl.BlockSpec((rb, D_VA), lambda i: (i, 0)),
                  pl.BlockSpec((sb, DK_A, DV_A), lambda i: (i, 0, 0))],
        out_specs=(pl.BlockSpec((rb, D_VA), lambda i: (i, 0)),
                   pl.BlockSpec((sb, DK_A, DV_A), lambda i: (i, 0, 0))),
        out_shape=(jax.ShapeDtypeStruct((n_rows, D_VA), F32),
                   jax.ShapeDtypeStruct(state.shape, F32)),
        compiler_params=pltpu.CompilerParams(dimension_semantics=("arbitrary",),
                                             vmem_limit_bytes=VMEM_LIMIT),
        name="sample_state",
    )(qi, kh, d, iv, state)


def _sample_post_kernel(x_ref, oin_ref, oint_ref, og_ref, gas_ref, hb_ref, hng_ref, wpa_ref, wo_ref,
                        gpost_ref, y_ref):
    y_ref[...] = _mix_out(x_ref[...], oin_ref[...] + oint_ref[...], og_ref[...], gas_ref[...], hb_ref[...],
                          hng_ref[...], wpa_ref, wo_ref, gpost_ref[...])


def _sample_post(x, oin, oint, og, gas, hb, hng, wpa, wo, gpost):
    return pl.pallas_call(
        _sample_post_kernel,
        out_shape=jax.ShapeDtypeStruct(x.shape, F32),
        compiler_params=pltpu.CompilerParams(vmem_limit_bytes=VMEM_LIMIT),
        name="sample_post",
    )(x, oin, oint, og, gas, hb, hng, wpa, wo, gpost)


def _sample_ffn_kernel(x_ref, gpre_ref, wug_ref, wuv_ref, cg_ref, cv_ref, cwg_ref, cwv_ref, cbg_ref, cbv_ref,
                       wdn_ref, gpost_ref, y_ref, csg_ref, csv_ref, xn_ref, acc_ref):
    blk = pl.program_id(0)
    n_rows = x_ref.shape[0]
    n_cache = cg_ref.shape[0]
    seq = 4

    @pl.when(blk == 0)
    def _():
        xn_ref[...] = _rms(x_ref[...], gpre_ref[...]).astype(BF16)
        acc_ref[...] = jnp.zeros_like(acc_ref)

    xn = xn_ref[...]
    tok = _mod(_iota((n_rows, FFN_CB), 0), seq)
    er, ec = _iota((2 * n_rows, n_cache), 0), _iota((2 * n_rows, n_cache), 1)
    r_seq, r_tok, c_seq, c_j = _div(_mod(er, n_rows), seq), _mod(er, seq), _div(ec, 2), _mod(ec, 2)
    src_j = jnp.where(er < n_rows, r_tok, jnp.where(r_tok == 0, 1, -1))
    expand = _ones_where((r_seq == c_seq) & (src_j == c_j))
    pr, pc = _iota((n_cache, n_rows), 0), _iota((n_cache, n_rows), 1)
    pick = _ones_where((_div(pr, 2) == _div(pc, seq)) & (_mod(pc, seq) == _mod(pr, 2) + 2))

    def conv(wu_ref, cache_ref, cw_ref, cb_ref, cs_ref):
        up = jnp.dot(xn, wu_ref[...], preferred_element_type=F32)
        cs_ref[...] = _dot_split(pick, up)
        old = _dot_split(expand, cache_ref[...])
        m2 = jnp.where(tok >= 2, pltpu.roll(up, 2, 0), old[:n_rows])
        m1 = jnp.where(tok >= 1, pltpu.roll(up, 1, 0), old[n_rows:])
        return cb_ref[...] + cw_ref[0:1, :] * m2 + cw_ref[1:2, :] * m1 + cw_ref[2:3, :] * up

    a = jax.nn.gelu(conv(wug_ref, cg_ref, cwg_ref, cbg_ref, csg_ref)) * conv(wuv_ref, cv_ref, cwv_ref, cbv_ref, csv_ref)
    acc_ref[...] += _dot(a, wdn_ref[...])

    @pl.when(blk == pl.num_programs(0) - 1)
    def _():
        y_ref[...] = x_ref[...] + _rms(acc_ref[...], gpost_ref[...])


def _sample_ffn(x, gpre, wup, cache, cw, cb, wdn, gpost):
    n_rows = x.shape[0]
    n_cache = cache.shape[0]
    nblk = D_FF // FFN_CB
    gate = lambda i: (0, i)
    val = lambda i: (0, i + nblk)
    full = lambda shape: pl.BlockSpec(shape, lambda i: (0,) * len(shape))
    return pl.pallas_call(
        _sample_ffn_kernel,
        grid=(nblk,),
        in_specs=[full((n_rows, D_MODEL)), full((1, D_MODEL)),
                  pl.BlockSpec((D_MODEL, FFN_CB), gate), pl.BlockSpec((D_MODEL, FFN_CB), val),
                  pl.BlockSpec((n_cache, FFN_CB), gate), pl.BlockSpec((n_cache, FFN_CB), val),
                  pl.BlockSpec((CONV_W, FFN_CB), gate), pl.BlockSpec((CONV_W, FFN_CB), val),
                  pl.BlockSpec((1, FFN_CB), gate), pl.BlockSpec((1, FFN_CB), val),
                  pl.BlockSpec((FFN_CB, D_MODEL), lambda i: (i, 0)), full((1, D_MODEL))],
        out_specs=(full((n_rows, D_MODEL)),
                   pl.BlockSpec((n_cache, FFN_CB), gate), pl.BlockSpec((n_cache, FFN_CB), gate)),
        out_shape=(jax.ShapeDtypeStruct((n_rows, D_MODEL), F32),
                   jax.ShapeDtypeStruct((n_cache, D_FF), F32),
                   jax.ShapeDtypeStruct((n_cache, D_FF), F32)),
        scratch_shapes=[pltpu.VMEM((n_rows, D_MODEL), BF16), pltpu.VMEM((n_rows, D_MODEL), F32)],
        compiler_params=pltpu.CompilerParams(dimension_semantics=("arbitrary",),
                                             vmem_limit_bytes=VMEM_LIMIT),
        name="sample_ffn",
    )(x, gpre, wup, wup, cache, cache, cw, cw, cb, cb, wdn, gpost)


def kernel(x_prompt, x_sample, state_hgrn, cache_ffn_conv, lb_param, mix_pre_g, w_in, hgrn_norm_g, gmlp_ln_g, gmlp_ln_b, w_s, b_s, w_pa, w_pb, w_o, mix_post_g, ffn_pre_g, w_up, conv_w, conv_b, w_down, ffn_post_g):
    nb_s, seq_s, _ = x_sample.shape
    row = lambda v: v.reshape(1, -1)
    win, wpa, wpb, wo = (w[0].astype(BF16) for w in (w_in, w_pa, w_pb, w_o))
    wup, wdn, ws = w_up[0].astype(BF16), w_down[0].astype(BF16), w_s[0].astype(BF16)
    gpre, gpost, fpre, fpost = row(mix_pre_g[0]), row(mix_post_g[0]), row(ffn_pre_g[0]), row(ffn_post_g[0])
    lng, lnb, cb = row(gmlp_ln_g[0]), row(gmlp_ln_b[0]), row(conv_b[0])
    hng = row(jnp.tile(hgrn_norm_g[0], H_A))
    bsb = jnp.broadcast_to(b_s[0][:, :, None], (H_B, GMLP_CHUNK, CH_B))

    x1, sp = _prompt_mixer(x_prompt, lb_param, gpre, win, hng, lng, lnb, ws, bsb, wpa, wpb, wo, gpost)
    yp, cp = _prompt_ffn(x1, fpre, wup, conv_w[0], cb, wdn, fpost)

    xs = x_sample.reshape(nb_s * seq_s, D_MODEL)
    wsm = w_s[0][:, :seq_s, :seq_s].reshape(-1)
    bsm = b_s[0][:, :seq_s].reshape(-1)
    qi, kh, d, iv, oin, og, gas, hb, vn = _sample_pre(wsm, bsm, xs, lb_param, gpre, win, lng, lnb, wpb)
    oint, ss = _sample_state(qi, kh, d, iv, state_hgrn[0].reshape(nb_s * H_A, DK_A, DV_A))
    xs1 = _sample_post(xs, oin, oint, og, gas, hb, hng, wpa, wo, gpost)
    cache = cache_ffn_conv[0].reshape(nb_s * (CONV_W - 1), 2 * D_FF)
    ys, csg, csv = _sample_ffn(xs1, fpre, wup, cache, conv_w[0], cb, wdn, fpost)
    cs = jnp.concatenate([csg, csv], axis=1).reshape(nb_s, CONV_W - 1, 2 * D_FF)

    return (yp, ys.reshape(nb_s, seq_s, D_MODEL), sp[None],
            ss.reshape(1, nb_s, H_A, DK_A, DV_A), cp[None], cs[None],
            vn.reshape(1, nb_s, seq_s, D_B))
```

```python
import functools

import jax
import jax.numpy as jnp
from jax import lax
from jax.experimental import pallas as pl
from jax.experimental.pallas import tpu as pltpu

F32 = jnp.float32
BF16 = jnp.bfloat16

D_MODEL = 1024
H_A, DK_A, DV_A = 8, 128, 64
D_QA, D_VA = H_A * DK_A, H_A * DV_A
H_B, CH_B = 4, 128
D_B = H_B * CH_B
GMLP_CHUNK = 128
D_FF = 2816
CONV_W = 3
EPS = 1e-6

OFF_Q, OFF_F, OFF_I, OFF_OG, OFF_U, OFF_V, OFF_GA, OFF_GB = 0, 1024, 2048, 2560, 3072, 3584, 4096, 5120
D_IN = 6144

HGRN_C = 64
PAIR_K = 2 * DK_A
PAIR_V = 2 * DV_A
N_PAIR = H_A // 2
T_MIX = 256
T_FFN = 512
FFN_CB = 256
SAMPLE_BB = 8
VMEM_LIMIT = 56 * 1024 * 1024


def _dot(a, b):
    return jnp.dot(a.astype(BF16), b.astype(BF16), preferred_element_type=F32)


def _dot_nt(a, b):
    return lax.dot_general(a.astype(BF16), b.astype(BF16), (((1,), (1,)), ((), ())),
                           preferred_element_type=F32)


def _dot_tn(a, b):
    return lax.dot_general(a.astype(BF16), b.astype(BF16), (((0,), (0,)), ((), ())),
                           preferred_element_type=F32)


def _dot_split(m, x):
    hi = x.astype(BF16)
    lo = (x - hi.astype(F32)).astype(BF16)
    m = m.astype(BF16)
    return (jnp.dot(m, hi, preferred_element_type=F32) + jnp.dot(m, lo, preferred_element_type=F32))


def _rms(x, g):
    return x * lax.rsqrt(jnp.mean(x * x, axis=-1, keepdims=True) + EPS) * g


def _layer_norm(x, g, b):
    xc = x - jnp.mean(x, axis=-1, keepdims=True)
    return xc * lax.rsqrt(jnp.mean(xc * xc, axis=-1, keepdims=True) + EPS) * g + b


def _lower_bound(lbp):
    e = jnp.exp(lbp - jnp.max(lbp, axis=0, keepdims=True))
    return e[0:1] / jnp.sum(e, axis=0, keepdims=True)


def _hgrn_features(q, f_logit, lb):
    f = lb + (1.0 - lb) * jax.nn.sigmoid(f_logit)
    return q * jax.nn.sigmoid(q), jnp.log(f), 1.0 - f


def _iota(shape, dim):
    return lax.broadcasted_iota(jnp.int32, shape, dim)


def _div(x, n):
    assert n & (n - 1) == 0
    return lax.shift_right_logical(x, n.bit_length() - 1)


def _mod(x, n):
    assert n & (n - 1) == 0
    return x & (n - 1)


def _ones_where(cond):
    return jnp.where(cond, 1.0, 0.0).astype(BF16)


def _head_mean_square(o):
    same_head = _ones_where(_div(_iota((D_VA, D_VA), 0), DV_A) == _div(_iota((D_VA, D_VA), 1), DV_A))
    return _dot_split_rhs(o * o, same_head) * (1.0 / DV_A)


def _dot_split_rhs(x, m):
    hi = x.astype(BF16)
    lo = (x - hi.astype(F32)).astype(BF16)
    return (jnp.dot(hi, m, preferred_element_type=F32) + jnp.dot(lo, m, preferred_element_type=F32))


def _mix_out(x, o, og, ga_sig, hb, hng, wpa_ref, wo_ref, gpost):
    oa = o * lax.rsqrt(_head_mean_square(o) + EPS) * hng * (og * jax.nn.sigmoid(og))
    h = ga_sig * _dot(oa, wpa_ref[...]) + hb
    return x + _rms(_dot(h, wo_ref[...]), gpost)


def _prompt_mixer_kernel(x_ref, lbp_ref, gpre_ref, win_ref, hng_ref, lng_ref, lnb_ref, ws_ref, bsb_ref,
                         wpa_ref, wpb_ref, wo_ref, gpost_ref, y_ref, sp_ref,
                         st_ref, qt_ref, kt_ref, qi_ref, kh_ref, iv_ref, d_ref, sc_ref, upd_ref, stb_ref,
                         o_ref, ob_ref):
    j = pl.program_id(1)
    t_tile = x_ref.shape[1]
    c = HGRN_C
    n_chunk = t_tile // c

    @pl.when(j == 0)
    def _():
        st_ref[...] = jnp.zeros_like(st_ref)

    x = x_ref[0]
    xn = _rms(x, gpre_ref[...]).astype(BF16)
    lb = _lower_bound(lbp_ref[...])

    def proj(off, width):
        return jnp.dot(xn, win_ref[:, off:off + width], preferred_element_type=F32)

    qf, logf, kk = _hgrn_features(proj(OFF_Q, D_QA), proj(OFF_F, D_QA), lb)
    iv_ref[...] = proj(OFF_I, D_VA).astype(BF16)
    r, s = _iota((c + 8, c), 0), _iota((c + 8, c), 1)
    cum_m = jnp.where(r < c, jnp.where(s <= r, 1.0, 0.0) - jnp.where(s < c // 2, 1.0, 0.0), 1.0)
    for ci in range(n_chunk):
        rows = slice(ci * c, (ci + 1) * c)
        lf = logf[rows]
        cum = _dot_split(cum_m, lf)
        a = cum[:c]
        b_last = cum[c:c + 1]
        b_mid = lf[0:1] - a[0:1]
        qt = qf[rows] * jnp.exp(a)
        kt = kk[rows] * jnp.exp(-a)
        qt_ref[rows, :] = qt.astype(BF16)
        kt_ref[rows, :] = kt.astype(BF16)
        qi_ref[rows, :] = (qt * jnp.exp(b_mid)).astype(BF16)
        kh_ref[rows, :] = (kt * jnp.exp(b_last - b_mid)).astype(BF16)
        d_ref[ci:ci + 1, :] = jnp.exp(b_last)

    kbd_mask = _div(_iota((2 * c, PAIR_K), 0), c) == _div(_iota((2 * c, PAIR_K), 1), DK_A)
    ibd_mask = _div(_iota((2 * c, PAIR_V), 0), c) == _div(_iota((2 * c, PAIR_V), 1), DV_A)
    sbd_mask = _div(_iota((PAIR_V, PAIR_K), 0), DV_A) == _div(_iota((PAIR_V, PAIR_K), 1), DK_A)
    causal = _mod(_iota((c, 2 * c), 1), c) <= _iota((c, 2 * c), 0)
    zero = jnp.zeros((), BF16)

    for p in range(N_PAIR):
        kcols = slice(p * PAIR_K, (p + 1) * PAIR_K)
        vcols = slice(p * PAIR_V, (p + 1) * PAIR_V)
        for ci in range(n_chunk):
            rows = slice(ci * c, (ci + 1) * c)
            kt = kt_ref[rows, kcols]
            kbd = jnp.where(kbd_mask, jnp.concatenate([kt, kt], axis=0), zero)
            sc = lax.dot_general(qt_ref[rows, kcols], kbd, (((1,), (1,)), ((), ())), preferred_element_type=F32)
            sc_ref[rows, vcols] = jnp.where(causal, sc, 0.0).astype(BF16)
            inc = lax.dot_general(iv_ref[rows, vcols], kh_ref[rows, kcols], (((0,), (0,)), ((), ())),
                                  preferred_element_type=F32)
            upd_ref[p * n_chunk + ci] = jnp.where(sbd_mask, inc, 0.0)

    for p in range(N_PAIR):
        kcols = slice(p * PAIR_K, (p + 1) * PAIR_K)
        st = st_ref[p]
        for ci in range(n_chunk):
            stb_ref[p * n_chunk + ci] = st.astype(BF16)
            st = st * d_ref[ci:ci + 1, kcols] + upd_ref[p * n_chunk + ci]
        st_ref[p] = st

    for p in range(N_PAIR):
        kcols = slice(p * PAIR_K, (p + 1) * PAIR_K)
        vcols = slice(p * PAIR_V, (p + 1) * PAIR_V)
        for ci in range(n_chunk):
            rows = slice(ci * c, (ci + 1) * c)
            ic = iv_ref[rows, vcols]
            ibd = jnp.where(ibd_mask, jnp.concatenate([ic, ic], axis=0), zero)
            o_ref[rows, vcols] = (
                jnp.dot(sc_ref[rows, vcols], ibd, preferred_element_type=F32)
                + lax.dot_general(qi_ref[rows, kcols], stb_ref[p * n_chunk + ci], (((1,), (1,)), ((), ())),
                                  preferred_element_type=F32))

    gu = jax.nn.gelu(proj(OFF_U, D_B))
    vn = _layer_norm(jax.nn.gelu(proj(OFF_V, D_B)), lng_ref[...], lnb_ref[...])
    tril = _iota((GMLP_CHUNK, GMLP_CHUNK), 1) <= _iota((GMLP_CHUNK, GMLP_CHUNK), 0)
    for g in range(H_B):
        w = jnp.where(tril, ws_ref[g], zero)
        cols = slice(g * CH_B, (g + 1) * CH_B)
        for n in range(t_tile // GMLP_CHUNK):
            rows = slice(n * GMLP_CHUNK, (n + 1) * GMLP_CHUNK)
            ob_ref[rows, cols] = gu[rows, cols] * (_dot(w, vn[rows, cols]) + bsb_ref[g])

    hb = jax.nn.sigmoid(proj(OFF_GB, D_MODEL)) * _dot(ob_ref[...], wpb_ref[...])
    y_ref[0] = _mix_out(x, o_ref[...], proj(OFF_OG, D_VA), jax.nn.sigmoid(proj(OFF_GA, D_MODEL)), hb,
                        hng_ref[...], wpa_ref, wo_ref, gpost_ref[...])

    @pl.when(j == pl.num_programs(1) - 1)
    def _():
        for h in range(H_A):
            hh = h % 2
            sp_ref[0, h] = st_ref[h // 2, hh * DV_A:(hh + 1) * DV_A, hh * DK_A:(hh + 1) * DK_A]


def _const_spec(shape):
    return pl.BlockSpec(shape, lambda *_: (0,) * len(shape), pipeline_mode=pl.Buffered(1))


def _prompt_mixer(x, lbp, gpre, win, hng, lng, lnb, ws, bsb, wpa, wpb, wo, gpost):
    nb, seq, _ = x.shape
    grid = (nb, seq // T_MIX)
    n_pc = N_PAIR * (T_MIX // HGRN_C)
    assert T_MIX // HGRN_C <= 8
    weights = (lbp, gpre, win, hng, lng, lnb, ws, bsb, wpa, wpb, wo, gpost)
    return pl.pallas_call(
        _prompt_mixer_kernel,
        grid=grid,
        in_specs=[pl.BlockSpec((1, T_MIX, D_MODEL), lambda b, j: (b, j, 0))]
        + [_const_spec(w.shape) for w in weights],
        out_specs=(pl.BlockSpec((1, T_MIX, D_MODEL), lambda b, j: (b, j, 0)),
                   pl.BlockSpec((1, H_A, DV_A, DK_A), lambda b, j: (b, 0, 0, 0))),
        out_shape=(jax.ShapeDtypeStruct(x.shape, F32),
                   jax.ShapeDtypeStruct((nb, H_A, DV_A, DK_A), F32)),
        scratch_shapes=[pltpu.VMEM((N_PAIR, PAIR_V, PAIR_K), F32),
                        pltpu.VMEM((T_MIX, D_QA), BF16),
                        pltpu.VMEM((T_MIX, D_QA), BF16),
                        pltpu.VMEM((T_MIX, D_QA), BF16),
                        pltpu.VMEM((T_MIX, D_QA), BF16),
                        pltpu.VMEM((T_MIX, D_VA), BF16),
                        pltpu.VMEM((8, D_QA), F32),
                        pltpu.VMEM((T_MIX, D_VA), BF16),
                        pltpu.VMEM((n_pc, PAIR_V, PAIR_K), F32),
                        pltpu.VMEM((n_pc, PAIR_V, PAIR_K), BF16),
                        pltpu.VMEM((T_MIX, D_VA), F32),
                        pltpu.VMEM((T_MIX, D_B), F32)],
        compiler_params=pltpu.CompilerParams(dimension_semantics=("arbitrary", "arbitrary"),
                                             vmem_limit_bytes=VMEM_LIMIT),
        name="prompt_mixer",
    )(x, *weights)


def _prompt_ffn_kernel(x_ref, gpre_ref, wup_ref, cw_ref, cb_ref, wdn_ref, gpost_ref, y_ref, cp_ref,
                       tail_ref, a_ref):
    j = pl.program_id(1)
    t_tile = x_ref.shape[1]

    @pl.when(j == 0)
    def _():
        tail_ref[...] = jnp.zeros_like(tail_ref)

    x = x_ref[0]
    xn = _rms(x, gpre_ref[...]).astype(BF16)
    row = _iota((t_tile, FFN_CB), 0)

    def conv(c0):
        cols = slice(c0, c0 + FFN_CB)
        up = jnp.dot(xn, wup_ref[:, cols], preferred_element_type=F32)
        p0 = tail_ref[6:7, cols]
        p1 = tail_ref[7:8, cols]
        tail_ref[:, cols] = up[t_tile - 8:]
        m1 = jnp.where(row == 0, p1, pltpu.roll(up, 1, 0))
        m2 = jnp.where(row == 0, p0, jnp.where(row == 1, p1, pltpu.roll(up, 2, 0)))
        return cb_ref[:, cols] + cw_ref[0:1, cols] * m2 + cw_ref[1:2, cols] * m1 + cw_ref[2:3, cols] * up

    for blk in range(D_FF // FFN_CB):
        c0 = blk * FFN_CB
        a_ref[:, c0:c0 + FFN_CB] = (jax.nn.gelu(conv(c0)) * conv(D_FF + c0)).astype(BF16)

    y_ref[0] = x + _rms(jnp.dot(a_ref[...], wdn_ref[...], preferred_element_type=F32), gpost_ref[...])

    @pl.when(j == pl.num_programs(1) - 1)
    def _():
        cp_ref[0] = tail_ref[6:8, :]


def _prompt_ffn(x, gpre, wup, cw, cb, wdn, gpost):
    nb, seq, _ = x.shape
    weights = (gpre, wup, cw, cb, wdn, gpost)
    return pl.pallas_call(
        _prompt_ffn_kernel,
        grid=(nb, seq // T_FFN),
        in_specs=[pl.BlockSpec((1, T_FFN, D_MODEL), lambda b, j: (b, j, 0))]
        + [_const_spec(w.shape) for w in weights],
        out_specs=(pl.BlockSpec((1, T_FFN, D_MODEL), lambda b, j: (b, j, 0)),
                   pl.BlockSpec((1, CONV_W - 1, 2 * D_FF), lambda b, j: (b, 0, 0))),
        out_shape=(jax.ShapeDtypeStruct(x.shape, F32),
                   jax.ShapeDtypeStruct((nb, CONV_W - 1, 2 * D_FF), F32)),
        scratch_shapes=[pltpu.VMEM((8, 2 * D_FF), F32),
                        pltpu.VMEM((T_FFN, D_FF), BF16)],
        compiler_params=pltpu.CompilerParams(dimension_semantics=("arbitrary", "arbitrary"),
                                             vmem_limit_bytes=VMEM_LIMIT),
        name="prompt_ffn",
    )(x, *weights)


def _sample_pre_kernel(wsm_ref, bsm_ref, x_ref, lbp_ref, gpre_ref, win_ref, lng_ref, lnb_ref, wpb_ref,
                       qi_ref, kh_ref, d_ref, i_ref, oin_ref, og_ref, gas_ref, hb_ref, vn_ref):
    n_rows = x_ref.shape[0]
    seq = 4
    x = x_ref[...]
    xn = _rms(x, gpre_ref[...]).astype(BF16)
    lb = _lower_bound(lbp_ref[...])

    def proj(off, width):
        return jnp.dot(xn, win_ref[:, off:off + width], preferred_element_type=F32)

    def tok(width):
        return _mod(_iota((n_rows, width), 0), seq)

    def shift(v, k):
        return pltpu.roll(v, k, 0) if k else v

    qf, logf, kk = _hgrn_features(proj(OFF_Q, D_QA), proj(OFF_F, D_QA), lb)
    iv = proj(OFF_I, D_VA)
    tq = tok(D_QA)
    b = logf + jnp.where(tq >= 1, shift(logf, 1), 0.0)
    b = b + jnp.where(tq >= 2, shift(b, 2), 0.0)
    b_last = jnp.where(tq == seq - 1, b, 0.0)
    b_last = b_last + pltpu.roll(b_last, n_rows - 1, 0)
    b_last = b_last + pltpu.roll(b_last, n_rows - 2, 0)
    qi_ref[...] = qf * jnp.exp(b)
    kh_ref[...] = kk * jnp.exp(b_last - b)
    d_ref[...] = jnp.exp(b_last)
    i_ref[...] = iv

    head_sum = _ones_where(_div(_iota((D_QA, D_VA), 0), DK_A) == _div(_iota((D_QA, D_VA), 1), DV_A))
    tv = tok(D_VA)
    o = jnp.zeros((n_rows, D_VA), F32)
    for k in range(seq):
        pair = qf * shift(kk, k) * jnp.exp(b - shift(b, k))
        sc = jnp.dot(jnp.where(tq >= k, pair, 0.0).astype(BF16), head_sum, preferred_element_type=F32)
        o = o + jnp.where(tv >= k, sc * shift(iv, k), 0.0)
    oin_ref[...] = o

    gu = jax.nn.gelu(proj(OFF_U, D_B))
    vn = _layer_norm(jax.nn.gelu(proj(OFF_V, D_B)), lng_ref[...], lnb_ref[...])
    vn_ref[...] = vn
    tg = tok(CH_B)
    obs = []
    for g in range(H_B):
        vg = vn[:, g * CH_B:(g + 1) * CH_B]
        s = jnp.zeros((n_rows, CH_B), F32)
        for t in range(seq):
            s = s + jnp.where(tg == t, bsm_ref[g * seq + t], 0.0)
        for k in range(seq):
            coef = jnp.zeros((n_rows, CH_B), F32)
            for t in range(k, seq):
                coef = coef + jnp.where(tg == t, wsm_ref[(g * seq + t) * seq + (t - k)], 0.0)
            s = s + coef * shift(vg, k)
        obs.append(gu[:, g * CH_B:(g + 1) * CH_B] * s)
    ob = jnp.concatenate(obs, axis=1)
    hb_ref[...] = jax.nn.sigmoid(proj(OFF_GB, D_MODEL)) * _dot(ob, wpb_ref[...])
    gas_ref[...] = jax.nn.sigmoid(proj(OFF_GA, D_MODEL))
    og_ref[...] = proj(OFF_OG, D_VA)


def _sample_pre(wsm, bsm, x, lbp, gpre, win, lng, lnb, wpb):
    n_rows = x.shape[0]
    shapes = [(n_rows, D_QA), (n_rows, D_QA), (n_rows, D_QA), (n_rows, D_VA), (n_rows, D_VA),
              (n_rows, D_VA), (n_rows, D_MODEL), (n_rows, D_MODEL), (n_rows, D_B)]
    smem = pl.BlockSpec(memory_space=pltpu.SMEM)
    return pl.pallas_call(
        _sample_pre_kernel,
        in_specs=[smem, smem] + [pl.BlockSpec(memory_space=pltpu.VMEM)] * 7,
        out_shape=tuple(jax.ShapeDtypeStruct(s, F32) for s in shapes),
        compiler_params=pltpu.CompilerParams(vmem_limit_bytes=VMEM_LIMIT),
        name="sample_pre",
    )(wsm, bsm, x, lbp, gpre, win, lng, lnb, wpb)


def _sample_state_kernel(qi_ref, kh_ref, d_ref, i_ref, s_ref, o_ref, sn_ref):
    seq = 4
    nseq = 4
    blk = nseq * seq
    rows_seq = _div(_iota((blk, DV_A), 0), seq)
    for m in range(qi_ref.shape[0] // blk):
        rows = slice(m * blk, (m + 1) * blk)
        outs = []
        for h in range(H_A):
            kcols = slice(h * DK_A, (h + 1) * DK_A)
            vcols = slice(h * DV_A, (h + 1) * DV_A)
            idx = [(m * nseq + bb) * H_A + h for bb in range(nseq)]
            st = [s_ref[i] for i in idx]
            q16 = qi_ref[rows, kcols].astype(BF16)
            r = lax.dot_general(q16, jnp.concatenate(st, axis=0).astype(BF16), (((1,), (1,)), ((), ())),
                                preferred_element_type=F32)
            oh = r[:, :DV_A]
            for bb in range(1, nseq):
                oh = jnp.where(rows_seq == bb, r[:, bb * DV_A:(bb + 1) * DV_A], oh)
            outs.append(oh)
            i16 = i_ref[rows, vcols]
            i_sel = jnp.concatenate([jnp.where(rows_seq == bb, i16, 0.0) for bb in range(nseq)], axis=1)
            inc = _dot_tn(i_sel, kh_ref[rows, kcols])
            for bb in range(nseq):
                d_row = d_ref[m * blk + bb * seq:m * blk + bb * seq + 1, kcols]
                sn_ref[idx[bb]] = st[bb] * d_row + inc[bb * DV_A:(bb + 1) * DV_A]
        o_ref[rows, :] = jnp.concatenate(outs, axis=1)


def _sample_state(qi, kh, d, iv, state):
    n_rows = qi.shape[0]
    rb = SAMPLE_BB * 4
    sb = SAMPLE_BB * H_A
    return pl.pallas_call(
        _sample_state_kernel,
        grid=(n_rows // rb,),
        in_specs=[pl.BlockSpec((rb, D_QA), lambda i: (i, 0)),
                  pl.BlockSpec((rb, D_QA), lambda i: (i, 0)),
                  pl.BlockSpec((rb, D_QA), lambda i: (i, 0)),
                  pl.BlockSpec((rb, D_VA), lambda i: (i, 0)),
                  pl.BlockSpec((sb, DV_A, DK_A), lambda i: (i, 0, 0))],
        out_specs=(pl.BlockSpec((rb, D_VA), lambda i: (i, 0)),
                   pl.BlockSpec((sb, DV_A, DK_A), lambda i: (i, 0, 0))),
        out_shape=(jax.ShapeDtypeStruct((n_rows, D_VA), F32),
                   jax.ShapeDtypeStruct(state.shape, F32)),
        compiler_params=pltpu.CompilerParams(dimension_semantics=("arbitrary",),
                                             vmem_limit_bytes=VMEM_LIMIT),
        name="sample_state",
    )(qi, kh, d, iv, state)


def _sample_post_kernel(x_ref, oin_ref, oint_ref, og_ref, gas_ref, hb_ref, hng_ref, wpa_ref, wo_ref,
                        gpost_ref, y_ref):
    y_ref[...] = _mix_out(x_ref[...], oin_ref[...] + oint_ref[...], og_ref[...], gas_ref[...], hb_ref[...],
                          hng_ref[...], wpa_ref, wo_ref, gpost_ref[...])


def _sample_post(x, oin, oint, og, gas, hb, hng, wpa, wo, gpost):
    return pl.pallas_call(
        _sample_post_kernel,
        out_shape=jax.ShapeDtypeStruct(x.shape, F32),
        compiler_params=pltpu.CompilerParams(vmem_limit_bytes=VMEM_LIMIT),
        name="sample_post",
    )(x, oin, oint, og, gas, hb, hng, wpa, wo, gpost)


def _sample_ffn_kernel(x_ref, gpre_ref, wug_ref, wuv_ref, cg_ref, cv_ref, cwg_ref, cwv_ref, cbg_ref, cbv_ref,
                       wdn_ref, gpost_ref, y_ref, csg_ref, csv_ref, xn_ref, acc_ref):
    blk = pl.program_id(0)
    n_rows = x_ref.shape[0]
    n_cache = cg_ref.shape[0]
    seq = 4

    @pl.when(blk == 0)
    def _():
        xn_ref[...] = _rms(x_ref[...], gpre_ref[...]).astype(BF16)
        acc_ref[...] = jnp.zeros_like(acc_ref)

    xn = xn_ref[...]
    tok = _mod(_iota((n_rows, FFN_CB), 0), seq)
    er, ec = _iota((2 * n_rows, n_cache), 0), _iota((2 * n_rows, n_cache), 1)
    r_seq, r_tok, c_seq, c_j = _div(_mod(er, n_rows), seq), _mod(er, seq), _div(ec, 2), _mod(ec, 2)
    src_j = jnp.where(er < n_rows, r_tok, jnp.where(r_tok == 0, 1, -1))
    expand = _ones_where((r_seq == c_seq) & (src_j == c_j))
    pr, pc = _iota((n_cache, n_rows), 0), _iota((n_cache, n_rows), 1)
    pick = _ones_where((_div(pr, 2) == _div(pc, seq)) & (_mod(pc, seq) == _mod(pr, 2) + 2))

    def conv(wu_ref, cache_ref, cw_ref, cb_ref, cs_ref):
        up = jnp.dot(xn, wu_ref[...], preferred_element_type=F32)
        cs_ref[...] = _dot_split(pick, up)
        old = _dot_split(expand, cache_ref[...])
        m2 = jnp.where(tok >= 2, pltpu.roll(up, 2, 0), old[:n_rows])
        m1 = jnp.where(tok >= 1, pltpu.roll(up, 1, 0), old[n_rows:])
        return cb_ref[...] + cw_ref[0:1, :] * m2 + cw_ref[1:2, :] * m1 + cw_ref[2:3, :] * up

    a = jax.nn.gelu(conv(wug_ref, cg_ref, cwg_ref, cbg_ref, csg_ref)) * conv(wuv_ref, cv_ref, cwv_ref, cbv_ref, csv_ref)
    acc_ref[...] += _dot(a, wdn_ref[...])

    @pl.when(blk == pl.num_programs(0) - 1)
    def _():
        y_ref[...] = x_ref[...] + _rms(acc_ref[...], gpost_ref[...])


def _sample_ffn(x, gpre, wup, cache, cw, cb, wdn, gpost):
    n_rows = x.shape[0]
    n_cache = cache.shape[0]
    nblk = D_FF // FFN_CB
    gate = lambda i: (0, i)
    val = lambda i: (0, i + nblk)
    full = lambda shape: pl.BlockSpec(shape, lambda i: (0,) * len(shape))
    return pl.pallas_call(
        _sample_ffn_kernel,
        grid=(nblk,),
        in_specs=[full((n_rows, D_MODEL)), full((1, D_MODEL)),
                  pl.BlockSpec((D_MODEL, FFN_CB), gate), pl.BlockSpec((D_MODEL, FFN_CB), val),
                  pl.BlockSpec((n_cache, FFN_CB), gate), pl.BlockSpec((n_cache, FFN_CB), val),
                  pl.BlockSpec((CONV_W, FFN_CB), gate), pl.BlockSpec((CONV_W, FFN_CB), val),
                  pl.BlockSpec((1, FFN_CB), gate), pl.BlockSpec((1, FFN_CB), val),
                  pl.BlockSpec((FFN_CB, D_MODEL), lambda i: (i, 0)), full((1, D_MODEL))],
        out_specs=(full((n_rows, D_MODEL)),
                   pl.BlockSpec((n_cache, FFN_CB), gate), pl.BlockSpec((n_cache, FFN_CB), gate)),
        out_shape=(jax.ShapeDtypeStruct((n_rows, D_MODEL), F32),
                   jax.ShapeDtypeStruct((n_cache, D_FF), F32),
                   jax.ShapeDtypeStruct((n_cache, D_FF), F32)),
        scratch_shapes=[pltpu.VMEM((n_rows, D_MODEL), BF16), pltpu.VMEM((n_rows, D_MODEL), F32)],
        compiler_params=pltpu.CompilerParams(dimension_semantics=("arbitrary",),
                                             vmem_limit_bytes=VMEM_LIMIT),
        name="sample_ffn",
    )(x, gpre, wup, wup, cache, cache, cw, cw, cb, cb, wdn, gpost)


def kernel(x_prompt, x_sample, state_hgrn, cache_ffn_conv, lb_param, mix_pre_g, w_in, hgrn_norm_g, gmlp_ln_g, gmlp_ln_b, w_s, b_s, w_pa, w_pb, w_o, mix_post_g, ffn_pre_g, w_up, conv_w, conv_b, w_down, ffn_post_g):
    nb_s, seq_s, _ = x_sample.shape
    row = lambda v: v.reshape(1, -1)
    win, wpa, wpb, wo = (w[0].astype(BF16) for w in (w_in, w_pa, w_pb, w_o))
    wup, wdn, ws = w_up[0].astype(BF16), w_down[0].astype(BF16), w_s[0].astype(BF16)
    gpre, gpost, fpre, fpost = row(mix_pre_g[0]), row(mix_post_g[0]), row(ffn_pre_g[0]), row(ffn_post_g[0])
    lng, lnb, cb = row(gmlp_ln_g[0]), row(gmlp_ln_b[0]), row(conv_b[0])
    hng = row(jnp.tile(hgrn_norm_g[0], H_A))
    bsb = jnp.broadcast_to(b_s[0][:, :, None], (H_B, GMLP_CHUNK, CH_B))

    x1, sp = _prompt_mixer(x_prompt, lb_param, gpre, win, hng, lng, lnb, ws, bsb, wpa, wpb, wo, gpost)
    yp, cp = _prompt_ffn(x1, fpre, wup, conv_w[0], cb, wdn, fpost)

    xs = x_sample.reshape(nb_s * seq_s, D_MODEL)
    wsm = w_s[0][:, :seq_s, :seq_s].reshape(-1)
    bsm = b_s[0][:, :seq_s].reshape(-1)
    qi, kh, d, iv, oin, og, gas, hb, vn = _sample_pre(wsm, bsm, xs, lb_param, gpre, win, lng, lnb, wpb)
    st_in = jnp.swapaxes(state_hgrn[0], -1, -2).reshape(nb_s * H_A, DV_A, DK_A)
    oint, ss = _sample_state(qi, kh, d, iv, st_in)
    xs1 = _sample_post(xs, oin, oint, og, gas, hb, hng, wpa, wo, gpost)
    cache = cache_ffn_conv[0].reshape(nb_s * (CONV_W - 1), 2 * D_FF)
    ys, csg, csv = _sample_ffn(xs1, fpre, wup, cache, conv_w[0], cb, wdn, fpost)
    cs = jnp.concatenate([csg, csv], axis=1).reshape(nb_s, CONV_W - 1, 2 * D_FF)

    return (yp, ys.reshape(nb_s, seq_s, D_MODEL), jnp.swapaxes(sp, -1, -2)[None],
            jnp.swapaxes(ss.reshape(1, nb_s, H_A, DV_A, DK_A), -1, -2), cp[None], cs[None],
            vn.reshape(1, nb_s, seq_s, D_B))
```

```python
import functools

import jax
import jax.numpy as jnp
from jax import lax
from jax.experimental import pallas as pl
from jax.experimental.pallas import tpu as pltpu

F32 = jnp.float32
BF16 = jnp.bfloat16

D_MODEL = 1024
H_A, DK_A, DV_A = 8, 128, 64
D_QA, D_VA = H_A * DK_A, H_A * DV_A
H_B, CH_B = 4, 128
D_B = H_B * CH_B
GMLP_CHUNK = 128
D_FF = 2816
CONV_W = 3
EPS = 1e-6

OFF_Q, OFF_F, OFF_I, OFF_OG, OFF_U, OFF_V, OFF_GA, OFF_GB = 0, 1024, 2048, 2560, 3072, 3584, 4096, 5120
D_IN = 6144

HGRN_C = 64
PAIR_K = 2 * DK_A
PAIR_V = 2 * DV_A
N_PAIR = H_A // 2
T_MIX = 512
T_FFN = 512
FFN_CB = 256
SAMPLE_BB = 32
VMEM_LIMIT = 56 * 1024 * 1024


def _dot(a, b):
    return jnp.dot(a.astype(BF16), b.astype(BF16), preferred_element_type=F32)


def _dot_nt(a, b):
    return lax.dot_general(a.astype(BF16), b.astype(BF16), (((1,), (1,)), ((), ())),
                           preferred_element_type=F32)


def _dot_tn(a, b):
    return lax.dot_general(a.astype(BF16), b.astype(BF16), (((0,), (0,)), ((), ())),
                           preferred_element_type=F32)


def _dot_split(m, x):
    hi = x.astype(BF16)
    lo = (x - hi.astype(F32)).astype(BF16)
    m = m.astype(BF16)
    return (jnp.dot(m, hi, preferred_element_type=F32) + jnp.dot(m, lo, preferred_element_type=F32))


def _rms(x, g):
    return x * lax.rsqrt(jnp.mean(x * x, axis=-1, keepdims=True) + EPS) * g


def _layer_norm(x, g, b):
    xc = x - jnp.mean(x, axis=-1, keepdims=True)
    return xc * lax.rsqrt(jnp.mean(xc * xc, axis=-1, keepdims=True) + EPS) * g + b


def _lower_bound(lbp):
    e = jnp.exp(lbp - jnp.max(lbp, axis=0, keepdims=True))
    return e[0:1] / jnp.sum(e, axis=0, keepdims=True)


def _hgrn_features(q, f_logit, lb):
    f = lb + (1.0 - lb) * jax.nn.sigmoid(f_logit)
    return q * jax.nn.sigmoid(q), jnp.log(f), 1.0 - f


def _iota(shape, dim):
    return lax.broadcasted_iota(jnp.int32, shape, dim)


def _div(x, n):
    assert n & (n - 1) == 0
    return lax.shift_right_logical(x, n.bit_length() - 1)


def _mod(x, n):
    assert n & (n - 1) == 0
    return x & (n - 1)


def _ones_where(cond):
    return jnp.where(cond, 1.0, 0.0).astype(BF16)


def _head_mean_square(o):
    same_head = _ones_where(_div(_iota((D_VA, D_VA), 0), DV_A) == _div(_iota((D_VA, D_VA), 1), DV_A))
    return _dot(o * o, same_head) * (1.0 / DV_A)


def _mix_out(x, o, og, ga_sig, hb, hng, wpa_ref, wo_ref, gpost):
    oa = o * lax.rsqrt(_head_mean_square(o) + EPS) * hng * (og * jax.nn.sigmoid(og))
    h = ga_sig * _dot(oa, wpa_ref[...]) + hb
    return x + _rms(_dot(h, wo_ref[...]), gpost)


def _prompt_mixer_kernel(x_ref, lbp_ref, gpre_ref, win_ref, hng_ref, lng_ref, lnb_ref, ws_ref, bsb_ref,
                         wpa_ref, wpb_ref, wo_ref, gpost_ref, y_ref, sp_ref,
                         st_ref, qt_ref, kt_ref, qi_ref, kh_ref, iv_ref, d_ref, sc_ref, upd_ref, stb_ref,
                         o_ref, ob_ref):
    j = pl.program_id(1)
    t_tile = x_ref.shape[1]
    c = HGRN_C
    n_chunk = t_tile // c

    @pl.when(j == 0)
    def _():
        st_ref[...] = jnp.zeros_like(st_ref)

    x = x_ref[0]
    xn = _rms(x, gpre_ref[...]).astype(BF16)
    lb = _lower_bound(lbp_ref[...])

    def proj(off, width):
        return jnp.dot(xn, win_ref[:, off:off + width], preferred_element_type=F32)

    qf, logf, kk = _hgrn_features(proj(OFF_Q, D_QA), proj(OFF_F, D_QA), lb)
    iv_ref[...] = proj(OFF_I, D_VA).astype(BF16)
    r, s = _iota((c + 8, c), 0), _iota((c + 8, c), 1)
    cum_m = jnp.where(r < c, jnp.where(s <= r, 1.0, 0.0) - jnp.where(s < c // 2, 1.0, 0.0), 1.0)
    for ci in range(n_chunk):
        rows = slice(ci * c, (ci + 1) * c)
        lf = logf[rows]
        cum = _dot_split(cum_m, lf)
        a = cum[:c]
        b_last = cum[c:c + 1]
        b_mid = lf[0:1] - a[0:1]
        qt = qf[rows] * jnp.exp(a)
        kt = kk[rows] * jnp.exp(-a)
        qt_ref[rows, :] = qt.astype(BF16)
        kt_ref[rows, :] = kt.astype(BF16)
        qi_ref[rows, :] = (qt * jnp.exp(b_mid)).astype(BF16)
        kh_ref[rows, :] = (kt * jnp.exp(b_last - b_mid)).astype(BF16)
        d_ref[ci:ci + 1, :] = jnp.exp(b_last)

    kbd_mask = _div(_iota((2 * c, PAIR_K), 0), c) == _div(_iota((2 * c, PAIR_K), 1), DK_A)
    ibd4_mask = _div(_iota((4 * c, 2 * PAIR_V), 0), c) == _div(_iota((4 * c, 2 * PAIR_V), 1), DV_A)
    sbd_mask = _div(_iota((PAIR_V, PAIR_K), 0), DV_A) == _div(_iota((PAIR_V, PAIR_K), 1), DK_A)
    causal = _mod(_iota((c, 2 * c), 1), c) <= _iota((c, 2 * c), 0)
    zero = jnp.zeros((), BF16)

    for p in range(N_PAIR):
        kcols = slice(p * PAIR_K, (p + 1) * PAIR_K)
        vcols = slice(p * PAIR_V, (p + 1) * PAIR_V)
        for ci in range(n_chunk):
            rows = slice(ci * c, (ci + 1) * c)
            kt = kt_ref[rows, kcols]
            kbd = jnp.where(kbd_mask, jnp.concatenate([kt, kt], axis=0), zero)
            sc = lax.dot_general(qt_ref[rows, kcols], kbd, (((1,), (1,)), ((), ())), preferred_element_type=F32)
            sc_ref[rows, vcols] = jnp.where(causal, sc, 0.0).astype(BF16)
            inc = lax.dot_general(iv_ref[rows, vcols], kh_ref[rows, kcols], (((0,), (0,)), ((), ())),
                                  preferred_element_type=F32)
            upd_ref[p * n_chunk + ci] = jnp.where(sbd_mask, inc, 0.0)

    for p in range(N_PAIR):
        kcols = slice(p * PAIR_K, (p + 1) * PAIR_K)
        st = st_ref[p]
        for ci in range(n_chunk):
            stb_ref[p * n_chunk + ci] = st.astype(BF16)
            st = st * d_ref[ci:ci + 1, kcols] + upd_ref[p * n_chunk + ci]
        st_ref[p] = st

    for q4 in range(N_PAIR // 2):
        vcols4 = slice(2 * q4 * PAIR_V, 2 * (q4 + 1) * PAIR_V)
        for ci in range(n_chunk):
            rows = slice(ci * c, (ci + 1) * c)
            ic4 = iv_ref[rows, vcols4]
            ibd4 = jnp.where(ibd4_mask, jnp.concatenate([ic4] * 4, axis=0), zero)
            inter = [lax.dot_general(qi_ref[rows, p * PAIR_K:(p + 1) * PAIR_K], stb_ref[p * n_chunk + ci],
                                     (((1,), (1,)), ((), ())), preferred_element_type=F32)
                     for p in (2 * q4, 2 * q4 + 1)]
            o_ref[rows, vcols4] = (jnp.dot(sc_ref[rows, vcols4], ibd4, preferred_element_type=F32)
                                   + jnp.concatenate(inter, axis=1))

    gu = jax.nn.gelu(proj(OFF_U, D_B))
    vn = _layer_norm(jax.nn.gelu(proj(OFF_V, D_B)), lng_ref[...], lnb_ref[...]).astype(BF16)
    tril = _iota((GMLP_CHUNK, GMLP_CHUNK), 1) <= _iota((GMLP_CHUNK, GMLP_CHUNK), 0)
    zblk = jnp.zeros((GMLP_CHUNK, CH_B), BF16)
    for g2 in range(H_B // 2):
        g0, g1 = 2 * g2, 2 * g2 + 1
        w2 = jnp.concatenate([jnp.where(tril, ws_ref[g0], zero), jnp.where(tril, ws_ref[g1], zero)], axis=1)
        b2 = jnp.concatenate([bsb_ref[g0], bsb_ref[g1]], axis=1)
        cols2 = slice(g0 * CH_B, (g1 + 1) * CH_B)
        for n in range(t_tile // GMLP_CHUNK):
            rows = slice(n * GMLP_CHUNK, (n + 1) * GMLP_CHUNK)
            v2 = vn[rows, cols2]
            vbd = jnp.concatenate([jnp.concatenate([v2[:, :CH_B], zblk], axis=1),
                                   jnp.concatenate([zblk, v2[:, CH_B:]], axis=1)], axis=0)
            ob_ref[rows, cols2] = gu[rows, cols2] * (jnp.dot(w2, vbd, preferred_element_type=F32) + b2)

    hb = jax.nn.sigmoid(proj(OFF_GB, D_MODEL)) * _dot(ob_ref[...], wpb_ref[...])
    y_ref[0] = _mix_out(x, o_ref[...], proj(OFF_OG, D_VA), jax.nn.sigmoid(proj(OFF_GA, D_MODEL)), hb,
                        hng_ref[...], wpa_ref, wo_ref, gpost_ref[...])

    @pl.when(j == pl.num_programs(1) - 1)
    def _():
        for h in range(H_A):
            hh = h % 2
            sp_ref[0, h] = st_ref[h // 2, hh * DV_A:(hh + 1) * DV_A, hh * DK_A:(hh + 1) * DK_A]


def _const_spec(shape):
    return pl.BlockSpec(shape, lambda *_: (0,) * len(shape), pipeline_mode=pl.Buffered(1))


def _prompt_mixer(x, lbp, gpre, win, hng, lng, lnb, ws, bsb, wpa, wpb, wo, gpost):
    nb, seq, _ = x.shape
    grid = (nb, seq // T_MIX)
    n_pc = N_PAIR * (T_MIX // HGRN_C)
    assert T_MIX // HGRN_C <= 8
    weights = (lbp, gpre, win, hng, lng, lnb, ws, bsb, wpa, wpb, wo, gpost)
    return pl.pallas_call(
        _prompt_mixer_kernel,
        grid=grid,
        in_specs=[pl.BlockSpec((1, T_MIX, D_MODEL), lambda b, j: (b, j, 0))]
        + [_const_spec(w.shape) for w in weights],
        out_specs=(pl.BlockSpec((1, T_MIX, D_MODEL), lambda b, j: (b, j, 0)),
                   pl.BlockSpec((1, H_A, DV_A, DK_A), lambda b, j: (b, 0, 0, 0))),
        out_shape=(jax.ShapeDtypeStruct(x.shape, F32),
                   jax.ShapeDtypeStruct((nb, H_A, DV_A, DK_A), F32)),
        scratch_shapes=[pltpu.VMEM((N_PAIR, PAIR_V, PAIR_K), F32),
                        pltpu.VMEM((T_MIX, D_QA), BF16),
                        pltpu.VMEM((T_MIX, D_QA), BF16),
                        pltpu.VMEM((T_MIX, D_QA), BF16),
                        pltpu.VMEM((T_MIX, D_QA), BF16),
                        pltpu.VMEM((T_MIX, D_VA), BF16),
                        pltpu.VMEM((8, D_QA), F32),
                        pltpu.VMEM((T_MIX, D_VA), BF16),
                        pltpu.VMEM((n_pc, PAIR_V, PAIR_K), F32),
                        pltpu.VMEM((n_pc, PAIR_V, PAIR_K), BF16),
                        pltpu.VMEM((T_MIX, D_VA), F32),
                        pltpu.VMEM((T_MIX, D_B), F32)],
        compiler_params=pltpu.CompilerParams(dimension_semantics=("arbitrary", "arbitrary"),
                                             vmem_limit_bytes=VMEM_LIMIT),
        name="prompt_mixer",
    )(x, *weights)


def _prompt_ffn_kernel(x_ref, gpre_ref, wup_ref, cw_ref, cb_ref, wdn_ref, gpost_ref, y_ref, cp_ref,
                       tail_ref, a_ref):
    j = pl.program_id(1)
    t_tile = x_ref.shape[1]

    @pl.when(j == 0)
    def _():
        tail_ref[...] = jnp.zeros_like(tail_ref)

    x = x_ref[0]
    xn = _rms(x, gpre_ref[...]).astype(BF16)
    row = _iota((t_tile, FFN_CB), 0)

    def up_proj(c0):
        return jnp.dot(xn, wup_ref[:, c0:c0 + FFN_CB], preferred_element_type=F32)

    def conv(c0, up):
        cols = slice(c0, c0 + FFN_CB)
        p0 = tail_ref[6:7, cols]
        p1 = tail_ref[7:8, cols]
        tail_ref[:, cols] = up[t_tile - 8:]
        m1 = jnp.where(row == 0, p1, pltpu.roll(up, 1, 0))
        m2 = jnp.where(row == 0, p0, jnp.where(row == 1, p1, pltpu.roll(up, 2, 0)))
        return cb_ref[:, cols] + cw_ref[0:1, cols] * m2 + cw_ref[1:2, cols] * m1 + cw_ref[2:3, cols] * up

    nblk = D_FF // FFN_CB
    ups = (up_proj(0), up_proj(D_FF))
    for blk in range(nblk):
        c0 = blk * FFN_CB
        nxt = (up_proj(c0 + FFN_CB), up_proj(D_FF + c0 + FFN_CB)) if blk + 1 < nblk else None
        a_ref[:, c0:c0 + FFN_CB] = (jax.nn.gelu(conv(c0, ups[0])) * conv(D_FF + c0, ups[1])).astype(BF16)
        ups = nxt

    y_ref[0] = x + _rms(jnp.dot(a_ref[...], wdn_ref[...], preferred_element_type=F32), gpost_ref[...])

    @pl.when(j == pl.num_programs(1) - 1)
    def _():
        cp_ref[0] = tail_ref[6:8, :]


def _prompt_ffn(x, gpre, wup, cw, cb, wdn, gpost):
    nb, seq, _ = x.shape
    weights = (gpre, wup, cw, cb, wdn, gpost)
    return pl.pallas_call(
        _prompt_ffn_kernel,
        grid=(nb, seq // T_FFN),
        in_specs=[pl.BlockSpec((1, T_FFN, D_MODEL), lambda b, j: (b, j, 0))]
        + [_const_spec(w.shape) for w in weights],
        out_specs=(pl.BlockSpec((1, T_FFN, D_MODEL), lambda b, j: (b, j, 0)),
                   pl.BlockSpec((1, CONV_W - 1, 2 * D_FF), lambda b, j: (b, 0, 0))),
        out_shape=(jax.ShapeDtypeStruct(x.shape, F32),
                   jax.ShapeDtypeStruct((nb, CONV_W - 1, 2 * D_FF), F32)),
        scratch_shapes=[pltpu.VMEM((8, 2 * D_FF), F32),
                        pltpu.VMEM((T_FFN, D_FF), BF16)],
        compiler_params=pltpu.CompilerParams(dimension_semantics=("arbitrary", "arbitrary"),
                                             vmem_limit_bytes=VMEM_LIMIT),
        name="prompt_ffn",
    )(x, *weights)


def _sample_pre_kernel(wsm_ref, bsm_ref, x_ref, lbp_ref, gpre_ref, win_ref, lng_ref, lnb_ref, wpb_ref,
                       qi_ref, kh_ref, d_ref, i_ref, oin_ref, og_ref, gas_ref, hb_ref, vn_ref):
    n_rows = x_ref.shape[0]
    seq = 4
    x = x_ref[...]
    xn = _rms(x, gpre_ref[...]).astype(BF16)
    lb = _lower_bound(lbp_ref[...])

    def proj(off, width):
        return jnp.dot(xn, win_ref[:, off:off + width], preferred_element_type=F32)

    def tok(width):
        return _mod(_iota((n_rows, width), 0), seq)

    def shift(v, k):
        return pltpu.roll(v, k, 0) if k else v

    qf, logf, kk = _hgrn_features(proj(OFF_Q, D_QA), proj(OFF_F, D_QA), lb)
    iv = proj(OFF_I, D_VA)
    tq = tok(D_QA)
    b = logf + jnp.where(tq >= 1, shift(logf, 1), 0.0)
    b = b + jnp.where(tq >= 2, shift(b, 2), 0.0)
    b_last = jnp.where(tq == seq - 1, b, 0.0)
    b_last = b_last + pltpu.roll(b_last, n_rows - 1, 0)
    b_last = b_last + pltpu.roll(b_last, n_rows - 2, 0)
    qi_ref[...] = qf * jnp.exp(b)
    kh_ref[...] = kk * jnp.exp(b_last - b)
    d_ref[...] = jnp.exp(b_last)
    i_ref[...] = iv

    head_sum = _ones_where(_div(_iota((D_QA, D_VA), 0), DK_A) == _div(_iota((D_QA, D_VA), 1), DV_A))
    tv = tok(D_VA)
    o = jnp.zeros((n_rows, D_VA), F32)
    for k in range(seq):
        pair = qf * shift(kk, k) * jnp.exp(b - shift(b, k))
        sc = jnp.dot(jnp.where(tq >= k, pair, 0.0).astype(BF16), head_sum, preferred_element_type=F32)
        o = o + jnp.where(tv >= k, sc * shift(iv, k), 0.0)
    oin_ref[...] = o

    gu = jax.nn.gelu(proj(OFF_U, D_B))
    vn = _layer_norm(jax.nn.gelu(proj(OFF_V, D_B)), lng_ref[...], lnb_ref[...])
    vn_ref[...] = vn
    tg = tok(CH_B)
    obs = []
    for g in range(H_B):
        vg = vn[:, g * CH_B:(g + 1) * CH_B]
        s = jnp.zeros((n_rows, CH_B), F32)
        for t in range(seq):
            s = s + jnp.where(tg == t, bsm_ref[g * seq + t], 0.0)
        for k in range(seq):
            coef = jnp.zeros((n_rows, CH_B), F32)
            for t in range(k, seq):
                coef = coef + jnp.where(tg == t, wsm_ref[(g * seq + t) * seq + (t - k)], 0.0)
            s = s + coef * shift(vg, k)
        obs.append(gu[:, g * CH_B:(g + 1) * CH_B] * s)
    ob = jnp.concatenate(obs, axis=1)
    hb_ref[...] = jax.nn.sigmoid(proj(OFF_GB, D_MODEL)) * _dot(ob, wpb_ref[...])
    gas_ref[...] = jax.nn.sigmoid(proj(OFF_GA, D_MODEL))
    og_ref[...] = proj(OFF_OG, D_VA)


def _sample_pre(wsm, bsm, x, lbp, gpre, win, lng, lnb, wpb):
    n_rows = x.shape[0]
    shapes = [(n_rows, D_QA), (n_rows, D_QA), (n_rows, D_QA), (n_rows, D_VA), (n_rows, D_VA),
              (n_rows, D_VA), (n_rows, D_MODEL), (n_rows, D_MODEL), (n_rows, D_B)]
    smem = pl.BlockSpec(memory_space=pltpu.SMEM)
    return pl.pallas_call(
        _sample_pre_kernel,
        in_specs=[smem, smem] + [pl.BlockSpec(memory_space=pltpu.VMEM)] * 7,
        out_shape=tuple(jax.ShapeDtypeStruct(s, F32) for s in shapes),
        compiler_params=pltpu.CompilerParams(vmem_limit_bytes=VMEM_LIMIT),
        name="sample_pre",
    )(wsm, bsm, x, lbp, gpre, win, lng, lnb, wpb)


def _sample_state_kernel(qi_ref, kh_ref, d_ref, i_ref, s_ref, o_ref, sn_ref):
    seq = 4
    nseq = 4
    blk = nseq * seq
    rows_seq = _div(_iota((blk, DV_A), 0), seq)

    @pl.loop(0, qi_ref.shape[0] // blk)
    def _(m):
        row0 = pl.multiple_of(m * blk, blk)
        rows = pl.ds(row0, blk)
        outs = []
        for h in range(H_A):
            kcols = slice(h * DK_A, (h + 1) * DK_A)
            vcols = slice(h * DV_A, (h + 1) * DV_A)
            idx = [(m * nseq + bb) * H_A + h for bb in range(nseq)]
            st = [s_ref[i] for i in idx]
            q16 = qi_ref[rows, kcols].astype(BF16)
            r = lax.dot_general(q16, jnp.concatenate(st, axis=0).astype(BF16), (((1,), (1,)), ((), ())),
                                preferred_element_type=F32)
            oh = r[:, :DV_A]
            for bb in range(1, nseq):
                oh = jnp.where(rows_seq == bb, r[:, bb * DV_A:(bb + 1) * DV_A], oh)
            outs.append(oh)
            i16 = i_ref[rows, vcols]
            i_sel = jnp.concatenate([jnp.where(rows_seq == bb, i16, 0.0) for bb in range(nseq)], axis=1)
            inc = _dot_tn(i_sel, kh_ref[rows, kcols])
            d16 = d_ref[rows, kcols]
            for bb in range(nseq):
                sn_ref[idx[bb]] = st[bb] * d16[bb * seq:bb * seq + 1] + inc[bb * DV_A:(bb + 1) * DV_A]
        o_ref[rows, :] = jnp.concatenate(outs, axis=1)


def _sample_state(qi, kh, d, iv, state):
    n_rows = qi.shape[0]
    rb = SAMPLE_BB * 4
    sb = SAMPLE_BB * H_A
    return pl.pallas_call(
        _sample_state_kernel,
        grid=(n_rows // rb,),
        in_specs=[pl.BlockSpec((rb, D_QA), lambda i: (i, 0)),
                  pl.BlockSpec((rb, D_QA), lambda i: (i, 0)),
                  pl.BlockSpec((rb, D_QA), lambda i: (i, 0)),
                  pl.BlockSpec((rb, D_VA), lambda i: (i, 0)),
                  pl.BlockSpec((sb, DV_A, DK_A), lambda i: (i, 0, 0))],
        out_specs=(pl.BlockSpec((rb, D_VA), lambda i: (i, 0)),
                   pl.BlockSpec((sb, DV_A, DK_A), lambda i: (i, 0, 0))),
        out_shape=(jax.ShapeDtypeStruct((n_rows, D_VA), F32),
                   jax.ShapeDtypeStruct(state.shape, F32)),
        compiler_params=pltpu.CompilerParams(dimension_semantics=("arbitrary",),
                                             vmem_limit_bytes=VMEM_LIMIT),
        name="sample_state",
    )(qi, kh, d, iv, state)


def _sample_post_kernel(x_ref, oin_ref, oint_ref, og_ref, gas_ref, hb_ref, hng_ref, wpa_ref, wo_ref,
                        gpost_ref, y_ref):
    y_ref[...] = _mix_out(x_ref[...], oin_ref[...] + oint_ref[...], og_ref[...], gas_ref[...], hb_ref[...],
                          hng_ref[...], wpa_ref, wo_ref, gpost_ref[...])


def _sample_post(x, oin, oint, og, gas, hb, hng, wpa, wo, gpost):
    return pl.pallas_call(
        _sample_post_kernel,
        out_shape=jax.ShapeDtypeStruct(x.shape, F32),
        compiler_params=pltpu.CompilerParams(vmem_limit_bytes=VMEM_LIMIT),
        name="sample_post",
    )(x, oin, oint, og, gas, hb, hng, wpa, wo, gpost)


def _sample_ffn_kernel(x_ref, gpre_ref, wug_ref, wuv_ref, cg_ref, cv_ref, cwg_ref, cwv_ref, cbg_ref, cbv_ref,
                       wdn_ref, gpost_ref, y_ref, csg_ref, csv_ref, xn_ref, acc_ref):
    blk = pl.program_id(0)
    n_rows = x_ref.shape[0]
    n_cache = cg_ref.shape[0]
    seq = 4

    @pl.when(blk == 0)
    def _():
        xn_ref[...] = _rms(x_ref[...], gpre_ref[...]).astype(BF16)
        acc_ref[...] = jnp.zeros_like(acc_ref)

    xn = xn_ref[...]
    tok = _mod(_iota((n_rows, FFN_CB), 0), seq)
    er, ec = _iota((2 * n_rows, n_cache), 0), _iota((2 * n_rows, n_cache), 1)
    r_seq, r_tok, c_seq, c_j = _div(_mod(er, n_rows), seq), _mod(er, seq), _div(ec, 2), _mod(ec, 2)
    src_j = jnp.where(er < n_rows, r_tok, jnp.where(r_tok == 0, 1, -1))
    expand = _ones_where((r_seq == c_seq) & (src_j == c_j))
    pr, pc = _iota((n_cache, n_rows), 0), _iota((n_cache, n_rows), 1)
    pick = _ones_where((_div(pr, 2) == _div(pc, seq)) & (_mod(pc, seq) == _mod(pr, 2) + 2))

    def conv(wu_ref, cache_ref, cw_ref, cb_ref, cs_ref):
        up = jnp.dot(xn, wu_ref[...], preferred_element_type=F32)
        cs_ref[...] = _dot_split(pick, up)
        old = _dot_split(expand, cache_ref[...])
        m2 = jnp.where(tok >= 2, pltpu.roll(up, 2, 0), old[:n_rows])
        m1 = jnp.where(tok >= 1, pltpu.roll(up, 1, 0), old[n_rows:])
        return cb_ref[...] + cw_ref[0:1, :] * m2 + cw_ref[1:2, :] * m1 + cw_ref[2:3, :] * up

    a = jax.nn.gelu(conv(wug_ref, cg_ref, cwg_ref, cbg_ref, csg_ref)) * conv(wuv_ref, cv_ref, cwv_ref, cbv_ref, csv_ref)
    acc_ref[...] += _dot(a, wdn_ref[...])

    @pl.when(blk == pl.num_programs(0) - 1)
    def _():
        y_ref[...] = x_ref[...] + _rms(acc_ref[...], gpost_ref[...])


def _sample_ffn(x, gpre, wup, cache, cw, cb, wdn, gpost):
    n_rows = x.shape[0]
    n_cache = cache.shape[0]
    nblk = D_FF // FFN_CB
    gate = lambda i: (0, i)
    val = lambda i: (0, i + nblk)
    full = lambda shape: pl.BlockSpec(shape, lambda i: (0,) * len(shape))
    return pl.pallas_call(
        _sample_ffn_kernel,
        grid=(nblk,),
        in_specs=[full((n_rows, D_MODEL)), full((1, D_MODEL)),
                  pl.BlockSpec((D_MODEL, FFN_CB), gate), pl.BlockSpec((D_MODEL, FFN_CB), val),
                  pl.BlockSpec((n_cache, FFN_CB), gate), pl.BlockSpec((n_cache, FFN_CB), val),
                  pl.BlockSpec((CONV_W, FFN_CB), gate), pl.BlockSpec((CONV_W, FFN_CB), val),
                  pl.BlockSpec((1, FFN_CB), gate), pl.BlockSpec((1, FFN_CB), val),
                  pl.BlockSpec((FFN_CB, D_MODEL), lambda i: (i, 0)), full((1, D_MODEL))],
        out_specs=(full((n_rows, D_MODEL)),
                   pl.BlockSpec((n_cache, FFN_CB), gate), pl.BlockSpec((n_cache, FFN_CB), gate)),
        out_shape=(jax.ShapeDtypeStruct((n_rows, D_MODEL), F32),
                   jax.ShapeDtypeStruct((n_cache, D_FF), F32),
                   jax.ShapeDtypeStruct((n_cache, D_FF), F32)),
        scratch_shapes=[pltpu.VMEM((n_rows, D_MODEL), BF16), pltpu.VMEM((n_rows, D_MODEL), F32)],
        compiler_params=pltpu.CompilerParams(dimension_semantics=("arbitrary",),
                                             vmem_limit_bytes=VMEM_LIMIT),
        name="sample_ffn",
    )(x, gpre, wup, wup, cache, cache, cw, cw, cb, cb, wdn, gpost)


def kernel(x_prompt, x_sample, state_hgrn, cache_ffn_conv, lb_param, mix_pre_g, w_in, hgrn_norm_g, gmlp_ln_g, gmlp_ln_b, w_s, b_s, w_pa, w_pb, w_o, mix_post_g, ffn_pre_g, w_up, conv_w, conv_b, w_down, ffn_post_g):
    nb_s, seq_s, _ = x_sample.shape
    row = lambda v: v.reshape(1, -1)
    win, wpa, wpb, wo = (w[0].astype(BF16) for w in (w_in, w_pa, w_pb, w_o))
    wup, wdn, ws = w_up[0].astype(BF16), w_down[0].astype(BF16), w_s[0].astype(BF16)
    gpre, gpost, fpre, fpost = row(mix_pre_g[0]), row(mix_post_g[0]), row(ffn_pre_g[0]), row(ffn_post_g[0])
    lng, lnb, cb = row(gmlp_ln_g[0]), row(gmlp_ln_b[0]), row(conv_b[0])
    hng = row(jnp.tile(hgrn_norm_g[0], H_A))
    bsb = jnp.broadcast_to(b_s[0][:, :, None], (H_B, GMLP_CHUNK, CH_B))

    x1, sp = _prompt_mixer(x_prompt, lb_param, gpre, win, hng, lng, lnb, ws, bsb, wpa, wpb, wo, gpost)
    yp, cp = _prompt_ffn(x1, fpre, wup, conv_w[0], cb, wdn, fpost)

    xs = x_sample.reshape(nb_s * seq_s, D_MODEL)
    wsm = w_s[0][:, :seq_s, :seq_s].reshape(-1)
    bsm = b_s[0][:, :seq_s].reshape(-1)
    qi, kh, d, iv, oin, og, gas, hb, vn = _sample_pre(wsm, bsm, xs, lb_param, gpre, win, lng, lnb, wpb)
    st_in = jnp.swapaxes(state_hgrn[0], -1, -2).reshape(nb_s * H_A, DV_A, DK_A)
    oint, ss = _sample_state(qi, kh, d, iv, st_in)
    xs1 = _sample_post(xs, oin, oint, og, gas, hb, hng, wpa, wo, gpost)
    cache = cache_ffn_conv[0].reshape(nb_s * (CONV_W - 1), 2 * D_FF)
    ys, csg, csv = _sample_ffn(xs1, fpre, wup, cache, conv_w[0], cb, wdn, fpost)
    cs = jnp.concatenate([csg, csv], axis=1).reshape(nb_s, CONV_W - 1, 2 * D_FF)

    return (yp, ys.reshape(nb_s, seq_s, D_MODEL), jnp.swapaxes(sp, -1, -2)[None],
            jnp.swapaxes(ss.reshape(1, nb_s, H_A, DV_A, DK_A), -1, -2), cp[None], cs[None],
            vn.reshape(1, nb_s, seq_s, D_B))
```

```python
import functools

import jax
import jax.numpy as jnp
from jax import lax
from jax.experimental import pallas as pl
from jax.experimental.pallas import tpu as pltpu

F32 = jnp.float32
BF16 = jnp.bfloat16

D_MODEL = 1024
H_A, DK_A, DV_A = 8, 128, 64
D_QA, D_VA = H_A * DK_A, H_A * DV_A
H_B, CH_B = 4, 128
D_B = H_B * CH_B
GMLP_CHUNK = 128
D_FF = 2816
CONV_W = 3
EPS = 1e-6

OFF_Q, OFF_F, OFF_I, OFF_OG, OFF_U, OFF_V, OFF_GA, OFF_GB = 0, 1024, 2048, 2560, 3072, 3584, 4096, 5120
D_IN = 6144

HGRN_C = 64
PAIR_K = 2 * DK_A
PAIR_V = 2 * DV_A
N_PAIR = H_A // 2
T_MIX = 512
T_FFN = 512
FFN_CB = 256
SAMPLE_FFN_CB = 1408
SAMPLE_BB = 32
VMEM_LIMIT = 56 * 1024 * 1024


def _dot(a, b):
    return jnp.dot(a.astype(BF16), b.astype(BF16), preferred_element_type=F32)


def _dot_nt(a, b):
    return lax.dot_general(a.astype(BF16), b.astype(BF16), (((1,), (1,)), ((), ())),
                           preferred_element_type=F32)


def _dot_tn(a, b):
    return lax.dot_general(a.astype(BF16), b.astype(BF16), (((0,), (0,)), ((), ())),
                           preferred_element_type=F32)


def _dot_split(m, x):
    hi = x.astype(BF16)
    lo = (x - hi.astype(F32)).astype(BF16)
    m = m.astype(BF16)
    return (jnp.dot(m, hi, preferred_element_type=F32) + jnp.dot(m, lo, preferred_element_type=F32))


def _rms(x, g):
    return x * lax.rsqrt(jnp.mean(x * x, axis=-1, keepdims=True) + EPS) * g


def _layer_norm(x, g, b):
    xc = x - jnp.mean(x, axis=-1, keepdims=True)
    return xc * lax.rsqrt(jnp.mean(xc * xc, axis=-1, keepdims=True) + EPS) * g + b


def _lower_bound(lbp):
    e = jnp.exp(lbp - jnp.max(lbp, axis=0, keepdims=True))
    return e[0:1] / jnp.sum(e, axis=0, keepdims=True)


def _hgrn_features(q, f_logit, lb):
    f = lb + (1.0 - lb) * jax.nn.sigmoid(f_logit)
    return q * jax.nn.sigmoid(q), jnp.log(f), 1.0 - f


def _iota(shape, dim):
    return lax.broadcasted_iota(jnp.int32, shape, dim)


def _div(x, n):
    assert n & (n - 1) == 0
    return lax.shift_right_logical(x, n.bit_length() - 1)


def _mod(x, n):
    assert n & (n - 1) == 0
    return x & (n - 1)


def _ones_where(cond):
    return jnp.where(cond, 1.0, 0.0).astype(BF16)


def _head_mean_square(o):
    same_head = _ones_where(_div(_iota((D_VA, D_VA), 0), DV_A) == _div(_iota((D_VA, D_VA), 1), DV_A))
    return _dot(o * o, same_head) * (1.0 / DV_A)


def _mix_out(x, o, og, ga_sig, hb, hng, wpa_ref, wo_ref, gpost):
    oa = o * lax.rsqrt(_head_mean_square(o) + EPS) * hng * (og * jax.nn.sigmoid(og))
    h = ga_sig * _dot(oa, wpa_ref[...]) + hb
    return x + _rms(_dot(h, wo_ref[...]), gpost)


def _prompt_mixer_kernel(x_ref, lbp_ref, gpre_ref, win_ref, hng_ref, lng_ref, lnb_ref, ws_ref, bsb_ref,
                         wpa_ref, wpb_ref, wo_ref, gpost_ref, y_ref, sp_ref,
                         st_ref, qt_ref, kt_ref, qi_ref, kh_ref, iv_ref, d_ref, sc_ref, upd_ref, stb_ref,
                         o_ref, ob_ref):
    j = pl.program_id(1)
    t_tile = x_ref.shape[1]
    c = HGRN_C
    n_chunk = t_tile // c

    @pl.when(j == 0)
    def _():
        st_ref[...] = jnp.zeros_like(st_ref)

    x = x_ref[0]
    xn = _rms(x, gpre_ref[...]).astype(BF16)
    lb = _lower_bound(lbp_ref[...])

    def proj(off, width):
        return jnp.dot(xn, win_ref[:, off:off + width], preferred_element_type=F32)

    qf, logf, kk = _hgrn_features(proj(OFF_Q, D_QA), proj(OFF_F, D_QA), lb)
    iv_ref[...] = proj(OFF_I, D_VA).astype(BF16)
    r, s = _iota((c + 8, c), 0), _iota((c + 8, c), 1)
    cum_m = jnp.where(r < c, jnp.where(s <= r, 1.0, 0.0) - jnp.where(s < c // 2, 1.0, 0.0), 1.0)
    for ci in range(n_chunk):
        rows = slice(ci * c, (ci + 1) * c)
        lf = logf[rows]
        cum = _dot_split(cum_m, lf)
        a = cum[:c]
        b_last = cum[c:c + 1]
        b_mid = lf[0:1] - a[0:1]
        qt = qf[rows] * jnp.exp(a)
        kt = kk[rows] * jnp.exp(-a)
        qt_ref[rows, :] = qt.astype(BF16)
        kt_ref[rows, :] = kt.astype(BF16)
        qi_ref[rows, :] = (qt * jnp.exp(b_mid)).astype(BF16)
        kh_ref[rows, :] = (kt * jnp.exp(b_last - b_mid)).astype(BF16)
        d_ref[ci:ci + 1, :] = jnp.exp(b_last)

    kbd_mask = _div(_iota((2 * c, PAIR_K), 0), c) == _div(_iota((2 * c, PAIR_K), 1), DK_A)
    ibd4_mask = _div(_iota((4 * c, 2 * PAIR_V), 0), c) == _div(_iota((4 * c, 2 * PAIR_V), 1), DV_A)
    sbd_mask = _div(_iota((PAIR_V, PAIR_K), 0), DV_A) == _div(_iota((PAIR_V, PAIR_K), 1), DK_A)
    causal = _mod(_iota((c, 2 * c), 1), c) <= _iota((c, 2 * c), 0)
    zero = jnp.zeros((), BF16)

    for p in range(N_PAIR):
        kcols = slice(p * PAIR_K, (p + 1) * PAIR_K)
        vcols = slice(p * PAIR_V, (p + 1) * PAIR_V)
        for ci in range(n_chunk):
            rows = slice(ci * c, (ci + 1) * c)
            kt = kt_ref[rows, kcols]
            kbd = jnp.where(kbd_mask, jnp.concatenate([kt, kt], axis=0), zero)
            sc = lax.dot_general(qt_ref[rows, kcols], kbd, (((1,), (1,)), ((), ())), preferred_element_type=F32)
            sc_ref[rows, vcols] = jnp.where(causal, sc, 0.0).astype(BF16)
            inc = lax.dot_general(iv_ref[rows, vcols], kh_ref[rows, kcols], (((0,), (0,)), ((), ())),
                                  preferred_element_type=F32)
            upd_ref[p * n_chunk + ci] = jnp.where(sbd_mask, inc, 0.0)

    for p in range(N_PAIR):
        kcols = slice(p * PAIR_K, (p + 1) * PAIR_K)
        st = st_ref[p]
        for ci in range(n_chunk):
            stb_ref[p * n_chunk + ci] = st.astype(BF16)
            st = st * d_ref[ci:ci + 1, kcols] + upd_ref[p * n_chunk + ci]
        st_ref[p] = st

    for q4 in range(N_PAIR // 2):
        vcols4 = slice(2 * q4 * PAIR_V, 2 * (q4 + 1) * PAIR_V)
        for ci in range(n_chunk):
            rows = slice(ci * c, (ci + 1) * c)
            ic4 = iv_ref[rows, vcols4]
            ibd4 = jnp.where(ibd4_mask, jnp.concatenate([ic4] * 4, axis=0), zero)
            inter = [lax.dot_general(qi_ref[rows, p * PAIR_K:(p + 1) * PAIR_K], stb_ref[p * n_chunk + ci],
                                     (((1,), (1,)), ((), ())), preferred_element_type=F32)
                     for p in (2 * q4, 2 * q4 + 1)]
            o_ref[rows, vcols4] = (jnp.dot(sc_ref[rows, vcols4], ibd4, preferred_element_type=F32)
                                   + jnp.concatenate(inter, axis=1))

    gu = jax.nn.gelu(proj(OFF_U, D_B))
    vn = _layer_norm(jax.nn.gelu(proj(OFF_V, D_B)), lng_ref[...], lnb_ref[...]).astype(BF16)
    tril = _iota((GMLP_CHUNK, GMLP_CHUNK), 1) <= _iota((GMLP_CHUNK, GMLP_CHUNK), 0)
    zblk = jnp.zeros((GMLP_CHUNK, CH_B), BF16)
    for g2 in range(H_B // 2):
        g0, g1 = 2 * g2, 2 * g2 + 1
        w2 = jnp.concatenate([jnp.where(tril, ws_ref[g0], zero), jnp.where(tril, ws_ref[g1], zero)], axis=1)
        b2 = jnp.concatenate([bsb_ref[g0], bsb_ref[g1]], axis=1)
        cols2 = slice(g0 * CH_B, (g1 + 1) * CH_B)
        for n in range(t_tile // GMLP_CHUNK):
            rows = slice(n * GMLP_CHUNK, (n + 1) * GMLP_CHUNK)
            v2 = vn[rows, cols2]
            vbd = jnp.concatenate([jnp.concatenate([v2[:, :CH_B], zblk], axis=1),
                                   jnp.concatenate([zblk, v2[:, CH_B:]], axis=1)], axis=0)
            ob_ref[rows, cols2] = gu[rows, cols2] * (jnp.dot(w2, vbd, preferred_element_type=F32) + b2)

    hb = jax.nn.sigmoid(proj(OFF_GB, D_MODEL)) * _dot(ob_ref[...], wpb_ref[...])
    y_ref[0] = _mix_out(x, o_ref[...], proj(OFF_OG, D_VA), jax.nn.sigmoid(proj(OFF_GA, D_MODEL)), hb,
                        hng_ref[...], wpa_ref, wo_ref, gpost_ref[...])

    @pl.when(j == pl.num_programs(1) - 1)
    def _():
        for h in range(H_A):
            hh = h % 2
            sp_ref[0, h] = st_ref[h // 2, hh * DV_A:(hh + 1) * DV_A, hh * DK_A:(hh + 1) * DK_A]


def _const_spec(shape):
    return pl.BlockSpec(shape, lambda *_: (0,) * len(shape), pipeline_mode=pl.Buffered(1))


def _prompt_mixer(x, lbp, gpre, win, hng, lng, lnb, ws, bsb, wpa, wpb, wo, gpost):
    nb, seq, _ = x.shape
    grid = (nb, seq // T_MIX)
    n_pc = N_PAIR * (T_MIX // HGRN_C)
    assert T_MIX // HGRN_C <= 8
    weights = (lbp, gpre, win, hng, lng, lnb, ws, bsb, wpa, wpb, wo, gpost)
    return pl.pallas_call(
        _prompt_mixer_kernel,
        grid=grid,
        in_specs=[pl.BlockSpec((1, T_MIX, D_MODEL), lambda b, j: (b, j, 0))]
        + [_const_spec(w.shape) for w in weights],
        out_specs=(pl.BlockSpec((1, T_MIX, D_MODEL), lambda b, j: (b, j, 0)),
                   pl.BlockSpec((1, H_A, DV_A, DK_A), lambda b, j: (b, 0, 0, 0))),
        out_shape=(jax.ShapeDtypeStruct(x.shape, F32),
                   jax.ShapeDtypeStruct((nb, H_A, DV_A, DK_A), F32)),
        scratch_shapes=[pltpu.VMEM((N_PAIR, PAIR_V, PAIR_K), F32),
                        pltpu.VMEM((T_MIX, D_QA), BF16),
                        pltpu.VMEM((T_MIX, D_QA), BF16),
                        pltpu.VMEM((T_MIX, D_QA), BF16),
                        pltpu.VMEM((T_MIX, D_QA), BF16),
                        pltpu.VMEM((T_MIX, D_VA), BF16),
                        pltpu.VMEM((8, D_QA), F32),
                        pltpu.VMEM((T_MIX, D_VA), BF16),
                        pltpu.VMEM((n_pc, PAIR_V, PAIR_K), F32),
                        pltpu.VMEM((n_pc, PAIR_V, PAIR_K), BF16),
                        pltpu.VMEM((T_MIX, D_VA), F32),
                        pltpu.VMEM((T_MIX, D_B), F32)],
        compiler_params=pltpu.CompilerParams(dimension_semantics=("arbitrary", "arbitrary"),
                                             vmem_limit_bytes=VMEM_LIMIT),
        name="prompt_mixer",
    )(x, *weights)


def _prompt_ffn_kernel(x_ref, gpre_ref, wup_ref, cw_ref, cb_ref, wdn_ref, gpost_ref, y_ref, cp_ref,
                       tail_ref, a_ref):
    j = pl.program_id(1)
    t_tile = x_ref.shape[1]

    @pl.when(j == 0)
    def _():
        tail_ref[...] = jnp.zeros_like(tail_ref)

    x = x_ref[0]
    xn = _rms(x, gpre_ref[...]).astype(BF16)
    row = _iota((t_tile, FFN_CB), 0)

    def up_proj(c0):
        return jnp.dot(xn, wup_ref[:, c0:c0 + FFN_CB], preferred_element_type=F32)

    def conv(c0, up):
        cols = slice(c0, c0 + FFN_CB)
        p0 = tail_ref[6:7, cols]
        p1 = tail_ref[7:8, cols]
        tail_ref[:, cols] = up[t_tile - 8:]
        m1 = jnp.where(row == 0, p1, pltpu.roll(up, 1, 0))
        m2 = jnp.where(row == 0, p0, jnp.where(row == 1, p1, pltpu.roll(up, 2, 0)))
        return cb_ref[:, cols] + cw_ref[0:1, cols] * m2 + cw_ref[1:2, cols] * m1 + cw_ref[2:3, cols] * up

    nblk = D_FF // FFN_CB
    ups = (up_proj(0), up_proj(D_FF))
    for blk in range(nblk):
        c0 = blk * FFN_CB
        nxt = (up_proj(c0 + FFN_CB), up_proj(D_FF + c0 + FFN_CB)) if blk + 1 < nblk else None
        a_ref[:, c0:c0 + FFN_CB] = (jax.nn.gelu(conv(c0, ups[0])) * conv(D_FF + c0, ups[1])).astype(BF16)
        ups = nxt

    y_ref[0] = x + _rms(jnp.dot(a_ref[...], wdn_ref[...], preferred_element_type=F32), gpost_ref[...])

    @pl.when(j == pl.num_programs(1) - 1)
    def _():
        cp_ref[0] = tail_ref[6:8, :]


def _prompt_ffn(x, gpre, wup, cw, cb, wdn, gpost):
    nb, seq, _ = x.shape
    weights = (gpre, wup, cw, cb, wdn, gpost)
    return pl.pallas_call(
        _prompt_ffn_kernel,
        grid=(nb, seq // T_FFN),
        in_specs=[pl.BlockSpec((1, T_FFN, D_MODEL), lambda b, j: (b, j, 0))]
        + [_const_spec(w.shape) for w in weights],
        out_specs=(pl.BlockSpec((1, T_FFN, D_MODEL), lambda b, j: (b, j, 0)),
                   pl.BlockSpec((1, CONV_W - 1, 2 * D_FF), lambda b, j: (b, 0, 0))),
        out_shape=(jax.ShapeDtypeStruct(x.shape, F32),
                   jax.ShapeDtypeStruct((nb, CONV_W - 1, 2 * D_FF), F32)),
        scratch_shapes=[pltpu.VMEM((8, 2 * D_FF), F32),
                        pltpu.VMEM((T_FFN, D_FF), BF16)],
        compiler_params=pltpu.CompilerParams(dimension_semantics=("arbitrary", "arbitrary"),
                                             vmem_limit_bytes=VMEM_LIMIT),
        name="prompt_ffn",
    )(x, *weights)


def _sample_pre_kernel(wsm_ref, bsm_ref, x_ref, lbp_ref, gpre_ref, win_ref, lng_ref, lnb_ref, wpb_ref,
                       qi_ref, kh_ref, d_ref, i_ref, oin_ref, og_ref, gas_ref, hb_ref, vn_ref):
    n_rows = x_ref.shape[0]
    seq = 4
    x = x_ref[...]
    xn = _rms(x, gpre_ref[...]).astype(BF16)
    lb = _lower_bound(lbp_ref[...])

    def proj(off, width):
        return jnp.dot(xn, win_ref[:, off:off + width], preferred_element_type=F32)

    def tok(width):
        return _mod(_iota((n_rows, width), 0), seq)

    def shift(v, k):
        return pltpu.roll(v, k, 0) if k else v

    qf, logf, kk = _hgrn_features(proj(OFF_Q, D_QA), proj(OFF_F, D_QA), lb)
    iv = proj(OFF_I, D_VA)
    tq = tok(D_QA)
    b = logf + jnp.where(tq >= 1, shift(logf, 1), 0.0)
    b = b + jnp.where(tq >= 2, shift(b, 2), 0.0)
    b_last = jnp.where(tq == seq - 1, b, 0.0)
    b_last = b_last + pltpu.roll(b_last, n_rows - 1, 0)
    b_last = b_last + pltpu.roll(b_last, n_rows - 2, 0)
    qi_ref[...] = qf * jnp.exp(b)
    kh_ref[...] = kk * jnp.exp(b_last - b)
    d_ref[...] = jnp.exp(b_last)
    i_ref[...] = iv

    head_sum = _ones_where(_div(_iota((D_QA, D_VA), 0), DK_A) == _div(_iota((D_QA, D_VA), 1), DV_A))
    tv = tok(D_VA)
    o = jnp.zeros((n_rows, D_VA), F32)
    for k in range(seq):
        pair = qf * shift(kk, k) * jnp.exp(b - shift(b, k))
        sc = jnp.dot(jnp.where(tq >= k, pair, 0.0).astype(BF16), head_sum, preferred_element_type=F32)
        o = o + jnp.where(tv >= k, sc * shift(iv, k), 0.0)
    oin_ref[...] = o

    gu = jax.nn.gelu(proj(OFF_U, D_B))
    vn = _layer_norm(jax.nn.gelu(proj(OFF_V, D_B)), lng_ref[...], lnb_ref[...])
    vn_ref[...] = vn
    tg = tok(CH_B)
    obs = []
    for g in range(H_B):
        vg = vn[:, g * CH_B:(g + 1) * CH_B]
        s = jnp.zeros((n_rows, CH_B), F32)
        for t in range(seq):
            s = s + jnp.where(tg == t, bsm_ref[g * seq + t], 0.0)
        for k in range(seq):
            coef = jnp.zeros((n_rows, CH_B), F32)
            for t in range(k, seq):
                coef = coef + jnp.where(tg == t, wsm_ref[(g * seq + t) * seq + (t - k)], 0.0)
            s = s + coef * shift(vg, k)
        obs.append(gu[:, g * CH_B:(g + 1) * CH_B] * s)
    ob = jnp.concatenate(obs, axis=1)
    hb_ref[...] = jax.nn.sigmoid(proj(OFF_GB, D_MODEL)) * _dot(ob, wpb_ref[...])
    gas_ref[...] = jax.nn.sigmoid(proj(OFF_GA, D_MODEL))
    og_ref[...] = proj(OFF_OG, D_VA)


def _sample_pre(wsm, bsm, x, lbp, gpre, win, lng, lnb, wpb):
    n_rows = x.shape[0]
    shapes = [(n_rows, D_QA), (n_rows, D_QA), (n_rows, D_QA), (n_rows, D_VA), (n_rows, D_VA),
              (n_rows, D_VA), (n_rows, D_MODEL), (n_rows, D_MODEL), (n_rows, D_B)]
    smem = pl.BlockSpec(memory_space=pltpu.SMEM)
    return pl.pallas_call(
        _sample_pre_kernel,
        in_specs=[smem, smem] + [pl.BlockSpec(memory_space=pltpu.VMEM)] * 7,
        out_shape=tuple(jax.ShapeDtypeStruct(s, F32) for s in shapes),
        compiler_params=pltpu.CompilerParams(vmem_limit_bytes=VMEM_LIMIT),
        name="sample_pre",
    )(wsm, bsm, x, lbp, gpre, win, lng, lnb, wpb)


def _sample_state_kernel(qi_ref, kh_ref, d_ref, i_ref, s_ref, o_ref, sn_ref):
    seq = 4
    nseq = 4
    blk = nseq * seq
    rows_seq = _div(_iota((blk, DV_A), 0), seq)

    @pl.loop(0, qi_ref.shape[0] // blk)
    def _(m):
        row0 = pl.multiple_of(m * blk, blk)
        rows = pl.ds(row0, blk)
        outs = []
        for h in range(H_A):
            kcols = slice(h * DK_A, (h + 1) * DK_A)
            vcols = slice(h * DV_A, (h + 1) * DV_A)
            idx = [(m * nseq + bb) * H_A + h for bb in range(nseq)]
            st = [s_ref[i] for i in idx]
            q16 = qi_ref[rows, kcols].astype(BF16)
            r = lax.dot_general(q16, jnp.concatenate(st, axis=0).astype(BF16), (((1,), (1,)), ((), ())),
                                preferred_element_type=F32)
            oh = r[:, :DV_A]
            for bb in range(1, nseq):
                oh = jnp.where(rows_seq == bb, r[:, bb * DV_A:(bb + 1) * DV_A], oh)
            outs.append(oh)
            i16 = i_ref[rows, vcols]
            i_sel = jnp.concatenate([jnp.where(rows_seq == bb, i16, 0.0) for bb in range(nseq)], axis=1)
            inc = _dot_tn(i_sel, kh_ref[rows, kcols])
            d16 = d_ref[rows, kcols]
            for bb in range(nseq):
                sn_ref[idx[bb]] = st[bb] * d16[bb * seq:bb * seq + 1] + inc[bb * DV_A:(bb + 1) * DV_A]
        o_ref[rows, :] = jnp.concatenate(outs, axis=1)


def _sample_state(qi, kh, d, iv, state):
    n_rows = qi.shape[0]
    rb = SAMPLE_BB * 4
    sb = SAMPLE_BB * H_A
    return pl.pallas_call(
        _sample_state_kernel,
        grid=(n_rows // rb,),
        in_specs=[pl.BlockSpec((rb, D_QA), lambda i: (i, 0)),
                  pl.BlockSpec((rb, D_QA), lambda i: (i, 0)),
                  pl.BlockSpec((rb, D_QA), lambda i: (i, 0)),
                  pl.BlockSpec((rb, D_VA), lambda i: (i, 0)),
                  pl.BlockSpec((sb, DV_A, DK_A), lambda i: (i, 0, 0))],
        out_specs=(pl.BlockSpec((rb, D_VA), lambda i: (i, 0)),
                   pl.BlockSpec((sb, DV_A, DK_A), lambda i: (i, 0, 0))),
        out_shape=(jax.ShapeDtypeStruct((n_rows, D_VA), F32),
                   jax.ShapeDtypeStruct(state.shape, F32)),
        compiler_params=pltpu.CompilerParams(dimension_semantics=("arbitrary",),
                                             vmem_limit_bytes=VMEM_LIMIT),
        name="sample_state",
    )(qi, kh, d, iv, state)


def _sample_post_kernel(x_ref, oin_ref, oint_ref, og_ref, gas_ref, hb_ref, hng_ref, wpa_ref, wo_ref,
                        gpost_ref, y_ref):
    y_ref[...] = _mix_out(x_ref[...], oin_ref[...] + oint_ref[...], og_ref[...], gas_ref[...], hb_ref[...],
                          hng_ref[...], wpa_ref, wo_ref, gpost_ref[...])


def _sample_post(x, oin, oint, og, gas, hb, hng, wpa, wo, gpost):
    return pl.pallas_call(
        _sample_post_kernel,
        out_shape=jax.ShapeDtypeStruct(x.shape, F32),
        compiler_params=pltpu.CompilerParams(vmem_limit_bytes=VMEM_LIMIT),
        name="sample_post",
    )(x, oin, oint, og, gas, hb, hng, wpa, wo, gpost)


def _sample_ffn_kernel(x_ref, gpre_ref, wu_ref, cache_ref, cw_ref, cb_ref, wdn_ref, gpost_ref, y_ref, cs_ref,
                       xn_ref, acc_ref, old_ref, up_ref, val_ref):
    step = pl.program_id(0)
    nblk = pl.num_programs(0) // 2
    n_rows = x_ref.shape[0]
    n_seq = cache_ref.shape[0]
    seq = n_rows // n_seq
    keep = CONV_W - 1

    @pl.when(step == 0)
    def _():
        xn_ref[...] = _rms(x_ref[...], gpre_ref[...]).astype(BF16)
        acc_ref[...] = jnp.zeros_like(acc_ref)
        old_ref[...] = jnp.zeros_like(old_ref)

    up = jnp.dot(xn_ref[...], wu_ref[...], preferred_element_type=F32)
    up_ref[...] = up
    for b in range(n_seq):
        old_ref[seq * b:seq * b + keep, :] = cache_ref[b]
        cs_ref[b] = up_ref[seq * (b + 1) - keep:seq * (b + 1), :]
    old = old_ref[...]
    tok = _mod(_iota(up.shape, 0), seq)
    m2 = jnp.where(tok >= 2, pltpu.roll(up, 2, 0), old)
    m1 = jnp.where(tok >= 1, pltpu.roll(up, 1, 0), pltpu.roll(old, n_rows - 1, 0))
    conv = cb_ref[...] + cw_ref[0:1, :] * m2 + cw_ref[1:2, :] * m1 + cw_ref[2:3, :] * up

    @pl.when(step < nblk)
    def _():
        val_ref[step] = conv

    @pl.when(step >= nblk)
    def _():
        a = jax.nn.gelu(conv) * val_ref[step - nblk]
        acc_ref[...] += _dot(a, wdn_ref[...])

    @pl.when(step == pl.num_programs(0) - 1)
    def _():
        y_ref[...] = x_ref[...] + _rms(acc_ref[...], gpost_ref[...])


def _sample_ffn(x, gpre, wup, cache, cw, cb, wdn, gpost):
    n_rows = x.shape[0]
    n_seq = cache.shape[0]
    nblk = D_FF // SAMPLE_FFN_CB
    col = lambda i: (0, lax.rem(i + nblk, 2 * nblk))
    col3 = lambda i: (0, 0, lax.rem(i + nblk, 2 * nblk))
    full = lambda shape: pl.BlockSpec(shape, lambda i: (0,) * len(shape))
    return pl.pallas_call(
        _sample_ffn_kernel,
        grid=(2 * nblk,),
        in_specs=[full((n_rows, D_MODEL)), full((1, D_MODEL)),
                  pl.BlockSpec((D_MODEL, SAMPLE_FFN_CB), col),
                  pl.BlockSpec((n_seq, CONV_W - 1, SAMPLE_FFN_CB), col3),
                  pl.BlockSpec((CONV_W, SAMPLE_FFN_CB), col),
                  pl.BlockSpec((1, SAMPLE_FFN_CB), col),
                  pl.BlockSpec((SAMPLE_FFN_CB, D_MODEL), lambda i: (jnp.maximum(i - nblk, 0), 0)),
                  full((1, D_MODEL))],
        out_specs=(full((n_rows, D_MODEL)),
                   pl.BlockSpec((n_seq, CONV_W - 1, SAMPLE_FFN_CB), col3)),
        out_shape=(jax.ShapeDtypeStruct((n_rows, D_MODEL), F32),
                   jax.ShapeDtypeStruct(cache.shape, F32)),
        scratch_shapes=[pltpu.VMEM((n_rows, D_MODEL), BF16), pltpu.VMEM((n_rows, D_MODEL), F32),
                        pltpu.VMEM((n_rows, SAMPLE_FFN_CB), F32), pltpu.VMEM((n_rows, SAMPLE_FFN_CB), F32),
                        pltpu.VMEM((nblk, n_rows, SAMPLE_FFN_CB), F32)],
        compiler_params=pltpu.CompilerParams(dimension_semantics=("arbitrary",),
                                             vmem_limit_bytes=VMEM_LIMIT),
        name="sample_ffn",
    )(x, gpre, wup, cache, cw, cb, wdn, gpost)


def kernel(x_prompt, x_sample, state_hgrn, cache_ffn_conv, lb_param, mix_pre_g, w_in, hgrn_norm_g, gmlp_ln_g, gmlp_ln_b, w_s, b_s, w_pa, w_pb, w_o, mix_post_g, ffn_pre_g, w_up, conv_w, conv_b, w_down, ffn_post_g):
    nb_s, seq_s, _ = x_sample.shape
    row = lambda v: v.reshape(1, -1)
    win, wpa, wpb, wo = (w[0].astype(BF16) for w in (w_in, w_pa, w_pb, w_o))
    wup, wdn, ws = w_up[0].astype(BF16), w_down[0].astype(BF16), w_s[0].astype(BF16)
    gpre, gpost, fpre, fpost = row(mix_pre_g[0]), row(mix_post_g[0]), row(ffn_pre_g[0]), row(ffn_post_g[0])
    lng, lnb, cb = row(gmlp_ln_g[0]), row(gmlp_ln_b[0]), row(conv_b[0])
    hng = row(jnp.tile(hgrn_norm_g[0], H_A))
    bsb = jnp.broadcast_to(b_s[0][:, :, None], (H_B, GMLP_CHUNK, CH_B))

    x1, sp = _prompt_mixer(x_prompt, lb_param, gpre, win, hng, lng, lnb, ws, bsb, wpa, wpb, wo, gpost)
    yp, cp = _prompt_ffn(x1, fpre, wup, conv_w[0], cb, wdn, fpost)

    xs = x_sample.reshape(nb_s * seq_s, D_MODEL)
    wsm = w_s[0][:, :seq_s, :seq_s].reshape(-1)
    bsm = b_s[0][:, :seq_s].reshape(-1)
    qi, kh, d, iv, oin, og, gas, hb, vn = _sample_pre(wsm, bsm, xs, lb_param, gpre, win, lng, lnb, wpb)
    st_in = jnp.swapaxes(state_hgrn[0], -1, -2).reshape(nb_s * H_A, DV_A, DK_A)
    oint, ss = _sample_state(qi, kh, d, iv, st_in)
    xs1 = _sample_post(xs, oin, oint, og, gas, hb, hng, wpa, wo, gpost)
    ys, cs = _sample_ffn(xs1, fpre, wup, cache_ffn_conv[0], conv_w[0], cb, wdn, fpost)

    return (yp, ys.reshape(nb_s, seq_s, D_MODEL), jnp.swapaxes(sp, -1, -2)[None],
            jnp.swapaxes(ss.reshape(1, nb_s, H_A, DV_A, DK_A), -1, -2), cp[None], cs[None],
            vn.reshape(1, nb_s, seq_s, D_B))
```

```python
import math

import jax
import jax.numpy as jnp
from jax import lax
from jax.experimental import pallas as pl
from jax.experimental.pallas import tpu as pltpu

F32 = jnp.float32
BF16 = jnp.bfloat16

D_MODEL = 1024
H_A, DK_A, DV_A = 8, 128, 64
D_QA, D_VA = H_A * DK_A, H_A * DV_A
H_B, CH_B = 4, 128
D_B = H_B * CH_B
GMLP_CHUNK = 128
D_FF = 2816
CONV_W = 3
EPS = 1e-6

OFF_Q, OFF_F, OFF_I, OFF_OG, OFF_U, OFF_V, OFF_GA, OFF_GB = 0, 1024, 2048, 2560, 3072, 3584, 4096, 5120
D_IN = 6144

HGRN_C = 64
PAIR_K = 2 * DK_A
PAIR_V = 2 * DV_A
N_PAIR = H_A // 2
T_MIX = 512
T_FFN = 1024
FFN_CB = 256
SAMPLE_FFN_CB = 1408
SAMPLE_BB = 32
VMEM_LIMIT = 56 * 1024 * 1024


def _dot(a, b):
    return jnp.dot(a.astype(BF16), b.astype(BF16), preferred_element_type=F32)


def _dot_nt(a, b):
    return lax.dot_general(a.astype(BF16), b.astype(BF16), (((1,), (1,)), ((), ())),
                           preferred_element_type=F32)


def _dot_tn(a, b):
    return lax.dot_general(a.astype(BF16), b.astype(BF16), (((0,), (0,)), ((), ())),
                           preferred_element_type=F32)


def _dot_split(m, x):
    hi = x.astype(BF16)
    lo = (x - hi.astype(F32)).astype(BF16)
    m = m.astype(BF16)
    return (jnp.dot(m, hi, preferred_element_type=F32) + jnp.dot(m, lo, preferred_element_type=F32))


def _rms(x, g):
    return x * lax.rsqrt(jnp.mean(x * x, axis=-1, keepdims=True) + EPS) * g


def _gelu_gate(x):
    k = -2.0 * math.sqrt(2.0 / math.pi) * math.log2(math.e)
    return 1.0 / (1.0 + jnp.exp2(x * (k + (k * 0.044715) * (x * x))))


def _gelu(x):
    return x * _gelu_gate(x)


def _gelu_mul(x, v):
    return (x * v) * _gelu_gate(x)


def _layer_norm(x, g, b):
    xc = x - jnp.mean(x, axis=-1, keepdims=True)
    return xc * lax.rsqrt(jnp.mean(xc * xc, axis=-1, keepdims=True) + EPS) * g + b


def _lower_bound(lbp):
    e = jnp.exp(lbp - jnp.max(lbp, axis=0, keepdims=True))
    return e[0:1] / jnp.sum(e, axis=0, keepdims=True)


def _hgrn_features(q, f_logit, lb):
    f = lb + (1.0 - lb) * jax.nn.sigmoid(f_logit)
    return q * jax.nn.sigmoid(q), jnp.log(f), 1.0 - f


def _iota(shape, dim):
    return lax.broadcasted_iota(jnp.int32, shape, dim)


def _div(x, n):
    assert n & (n - 1) == 0
    return lax.shift_right_logical(x, n.bit_length() - 1)


def _mod(x, n):
    assert n & (n - 1) == 0
    return x & (n - 1)


def _ones_where(cond):
    return jnp.where(cond, 1.0, 0.0).astype(BF16)


def _head_mean_square(o):
    same_head = _ones_where(_div(_iota((D_VA, D_VA), 0), DV_A) == _div(_iota((D_VA, D_VA), 1), DV_A))
    return _dot(o * o, same_head) * (1.0 / DV_A)


def _mix_out(x, o, og, ga_sig, hb, hng, wpa_ref, wo_ref, gpost):
    oa = o * lax.rsqrt(_head_mean_square(o) + EPS) * hng * (og * jax.nn.sigmoid(og))
    h = ga_sig * _dot(oa, wpa_ref[...]) + hb
    return x + _rms(_dot(h, wo_ref[...]), gpost)


def _prompt_mixer_kernel(x_ref, lbp_ref, gpre_ref, win_ref, hng_ref, lng_ref, lnb_ref, ws_ref, bsb_ref,
                         wpa_ref, wpb_ref, wo_ref, gpost_ref, y_ref, sp_ref,
                         st_ref, qt_ref, kt_ref, qi_ref, kh_ref, iv_ref, d_ref, sc_ref, upd_ref, stb_ref,
                         o_ref, ob_ref):
    j = pl.program_id(1)
    t_tile = x_ref.shape[1]
    c = HGRN_C
    n_chunk = t_tile // c

    @pl.when(j == 0)
    def _():
        st_ref[...] = jnp.zeros_like(st_ref)

    x = x_ref[0]
    xn = _rms(x, gpre_ref[...]).astype(BF16)
    lb = _lower_bound(lbp_ref[...])

    def proj(off, width):
        return jnp.dot(xn, win_ref[:, off:off + width], preferred_element_type=F32)

    q_raw, f_raw = proj(OFF_Q, D_QA), proj(OFF_F, D_QA)
    iv_ref[...] = proj(OFF_I, D_VA).astype(BF16)
    u_raw, v_raw, og_raw = proj(OFF_U, D_B), proj(OFF_V, D_B), proj(OFF_OG, D_VA)
    ga_raw, gb_raw = proj(OFF_GA, D_MODEL), proj(OFF_GB, D_MODEL)
    qf, logf, kk = _hgrn_features(q_raw, f_raw, lb)
    r, s = _iota((c + 8, c), 0), _iota((c + 8, c), 1)
    cum_m = jnp.where(r < c, jnp.where(s <= r, 1.0, 0.0) - jnp.where(s < c // 2, 1.0, 0.0), 1.0)
    for ci in range(n_chunk):
        rows = slice(ci * c, (ci + 1) * c)
        lf = logf[rows]
        cum = _dot_split(cum_m, lf)
        a = cum[:c]
        b_last = cum[c:c + 1]
        b_mid = lf[0:1] - a[0:1]
        qt = qf[rows] * jnp.exp(a)
        kt = kk[rows] * jnp.exp(-a)
        qt_ref[rows, :] = qt.astype(BF16)
        kt_ref[rows, :] = kt.astype(BF16)
        qi_ref[rows, :] = (qt * jnp.exp(b_mid)).astype(BF16)
        kh_ref[rows, :] = (kt * jnp.exp(b_last - b_mid)).astype(BF16)
        d_ref[ci:ci + 1, :] = jnp.exp(b_last)

    kbd_mask = _div(_iota((2 * c, PAIR_K), 0), c) == _div(_iota((2 * c, PAIR_K), 1), DK_A)
    ibd4_mask = _div(_iota((4 * c, 2 * PAIR_V), 0), c) == _div(_iota((4 * c, 2 * PAIR_V), 1), DV_A)
    sbd_mask = _div(_iota((PAIR_V, PAIR_K), 0), DV_A) == _div(_iota((PAIR_V, PAIR_K), 1), DK_A)
    causal = _mod(_iota((c, 2 * c), 1), c) <= _iota((c, 2 * c), 0)
    zero = jnp.zeros((), BF16)

    for p in range(N_PAIR):
        kcols = slice(p * PAIR_K, (p + 1) * PAIR_K)
        vcols = slice(p * PAIR_V, (p + 1) * PAIR_V)
        for ci in range(n_chunk):
            rows = slice(ci * c, (ci + 1) * c)
            kt = kt_ref[rows, kcols]
            kbd = jnp.where(kbd_mask, jnp.concatenate([kt, kt], axis=0), zero)
            sc = lax.dot_general(qt_ref[rows, kcols], kbd, (((1,), (1,)), ((), ())), preferred_element_type=F32)
            sc_ref[rows, vcols] = jnp.where(causal, sc, 0.0).astype(BF16)
            inc = lax.dot_general(iv_ref[rows, vcols], kh_ref[rows, kcols], (((0,), (0,)), ((), ())),
                                  preferred_element_type=F32)
            upd_ref[p * n_chunk + ci] = jnp.where(sbd_mask, inc, 0.0)

    for p in range(N_PAIR):
        kcols = slice(p * PAIR_K, (p + 1) * PAIR_K)
        st = st_ref[p]
        for ci in range(n_chunk):
            stb_ref[p * n_chunk + ci] = st.astype(BF16)
            st = st * d_ref[ci:ci + 1, kcols] + upd_ref[p * n_chunk + ci]
        st_ref[p] = st

    for q4 in range(N_PAIR // 2):
        vcols4 = slice(2 * q4 * PAIR_V, 2 * (q4 + 1) * PAIR_V)
        for ci in range(n_chunk):
            rows = slice(ci * c, (ci + 1) * c)
            ic4 = iv_ref[rows, vcols4]
            ibd4 = jnp.where(ibd4_mask, jnp.concatenate([ic4] * 4, axis=0), zero)
            inter = [lax.dot_general(qi_ref[rows, p * PAIR_K:(p + 1) * PAIR_K], stb_ref[p * n_chunk + ci],
                                     (((1,), (1,)), ((), ())), preferred_element_type=F32)
                     for p in (2 * q4, 2 * q4 + 1)]
            o_ref[rows, vcols4] = (jnp.dot(sc_ref[rows, vcols4], ibd4, preferred_element_type=F32)
                                   + jnp.concatenate(inter, axis=1))

    gu = _gelu(u_raw)
    vn = _layer_norm(_gelu(v_raw), lng_ref[...], lnb_ref[...]).astype(BF16)
    tril = _iota((GMLP_CHUNK, GMLP_CHUNK), 1) <= _iota((GMLP_CHUNK, GMLP_CHUNK), 0)
    zblk = jnp.zeros((GMLP_CHUNK, CH_B), BF16)
    for g2 in range(H_B // 2):
        g0, g1 = 2 * g2, 2 * g2 + 1
        w2 = jnp.concatenate([jnp.where(tril, ws_ref[g0], zero), jnp.where(tril, ws_ref[g1], zero)], axis=1)
        b2 = jnp.concatenate([bsb_ref[g0], bsb_ref[g1]], axis=1)
        cols2 = slice(g0 * CH_B, (g1 + 1) * CH_B)
        for n in range(t_tile // GMLP_CHUNK):
            rows = slice(n * GMLP_CHUNK, (n + 1) * GMLP_CHUNK)
            v2 = vn[rows, cols2]
            vbd = jnp.concatenate([jnp.concatenate([v2[:, :CH_B], zblk], axis=1),
                                   jnp.concatenate([zblk, v2[:, CH_B:]], axis=1)], axis=0)
            ob_ref[rows, cols2] = gu[rows, cols2] * (jnp.dot(w2, vbd, preferred_element_type=F32) + b2)

    hb = jax.nn.sigmoid(gb_raw) * _dot(ob_ref[...], wpb_ref[...])
    y_ref[0] = _mix_out(x, o_ref[...], og_raw, jax.nn.sigmoid(ga_raw), hb,
                        hng_ref[...], wpa_ref, wo_ref, gpost_ref[...])

    @pl.when(j == pl.num_programs(1) - 1)
    def _():
        for h in range(H_A):
            hh = h % 2
            sp_ref[0, h] = st_ref[h // 2, hh * DV_A:(hh + 1) * DV_A, hh * DK_A:(hh + 1) * DK_A]


def _const_spec(shape):
    return pl.BlockSpec(shape, lambda *_: (0,) * len(shape), pipeline_mode=pl.Buffered(1))


def _prompt_mixer(x, lbp, gpre, win, hng, lng, lnb, ws, bsb, wpa, wpb, wo, gpost):
    nb, seq, _ = x.shape
    grid = (nb, seq // T_MIX)
    n_pc = N_PAIR * (T_MIX // HGRN_C)
    assert T_MIX // HGRN_C <= 8
    weights = (lbp, gpre, win, hng, lng, lnb, ws, bsb, wpa, wpb, wo, gpost)
    return pl.pallas_call(
        _prompt_mixer_kernel,
        grid=grid,
        in_specs=[pl.BlockSpec((1, T_MIX, D_MODEL), lambda b, j: (b, j, 0))]
        + [_const_spec(w.shape) for w in weights],
        out_specs=(pl.BlockSpec((1, T_MIX, D_MODEL), lambda b, j: (b, j, 0)),
                   pl.BlockSpec((1, H_A, DV_A, DK_A), lambda b, j: (b, 0, 0, 0))),
        out_shape=(jax.ShapeDtypeStruct(x.shape, F32),
                   jax.ShapeDtypeStruct((nb, H_A, DV_A, DK_A), F32)),
        scratch_shapes=[pltpu.VMEM((N_PAIR, PAIR_V, PAIR_K), F32),
                        pltpu.VMEM((T_MIX, D_QA), BF16),
                        pltpu.VMEM((T_MIX, D_QA), BF16),
                        pltpu.VMEM((T_MIX, D_QA), BF16),
                        pltpu.VMEM((T_MIX, D_QA), BF16),
                        pltpu.VMEM((T_MIX, D_VA), BF16),
                        pltpu.VMEM((8, D_QA), F32),
                        pltpu.VMEM((T_MIX, D_VA), BF16),
                        pltpu.VMEM((n_pc, PAIR_V, PAIR_K), F32),
                        pltpu.VMEM((n_pc, PAIR_V, PAIR_K), BF16),
                        pltpu.VMEM((T_MIX, D_VA), F32),
                        pltpu.VMEM((T_MIX, D_B), F32)],
        compiler_params=pltpu.CompilerParams(dimension_semantics=("arbitrary", "arbitrary"),
                                             vmem_limit_bytes=VMEM_LIMIT),
        name="prompt_mixer",
    )(x, *weights)


def _prompt_ffn_kernel(x_ref, gpre_ref, wup_ref, cw_ref, cb_ref, wdn_ref, gpost_ref, y_ref, cp_ref,
                       tail_ref, a_ref):
    j = pl.program_id(1)
    t_tile = x_ref.shape[1]

    @pl.when(j == 0)
    def _():
        tail_ref[...] = jnp.zeros_like(tail_ref)

    x = x_ref[0]
    xn = _rms(x, gpre_ref[...]).astype(BF16)
    row = _iota((t_tile, FFN_CB), 0)

    def up_proj(c0):
        return jnp.dot(xn, wup_ref[:, c0:c0 + FFN_CB], preferred_element_type=F32)

    def conv(c0, up):
        cols = slice(c0, c0 + FFN_CB)
        p0 = tail_ref[6:7, cols]
        p1 = tail_ref[7:8, cols]
        tail_ref[:, cols] = up[t_tile - 8:]
        m1 = jnp.where(row == 0, p1, pltpu.roll(up, 1, 0))
        m2 = jnp.where(row == 0, p0, jnp.where(row == 1, p1, pltpu.roll(up, 2, 0)))
        return cb_ref[:, cols] + cw_ref[0:1, cols] * m2 + cw_ref[1:2, cols] * m1 + cw_ref[2:3, cols] * up

    nblk = D_FF // FFN_CB
    ups = (up_proj(0), up_proj(D_FF))
    for blk in range(nblk):
        c0 = blk * FFN_CB
        nxt = (up_proj(c0 + FFN_CB), up_proj(D_FF + c0 + FFN_CB)) if blk + 1 < nblk else None
        a_ref[:, c0:c0 + FFN_CB] = _gelu_mul(conv(c0, ups[0]), conv(D_FF + c0, ups[1])).astype(BF16)
        ups = nxt

    y_ref[0] = x + _rms(jnp.dot(a_ref[...], wdn_ref[...], preferred_element_type=F32), gpost_ref[...])

    @pl.when(j == pl.num_programs(1) - 1)
    def _():
        cp_ref[0] = tail_ref[6:8, :]


def _prompt_ffn(x, gpre, wup, cw, cb, wdn, gpost):
    nb, seq, _ = x.shape
    weights = (gpre, wup, cw, cb, wdn, gpost)
    return pl.pallas_call(
        _prompt_ffn_kernel,
        grid=(nb, seq // T_FFN),
        in_specs=[pl.BlockSpec((1, T_FFN, D_MODEL), lambda b, j: (b, j, 0))]
        + [_const_spec(w.shape) for w in weights],
        out_specs=(pl.BlockSpec((1, T_FFN, D_MODEL), lambda b, j: (b, j, 0)),
                   pl.BlockSpec((1, CONV_W - 1, 2 * D_FF), lambda b, j: (b, 0, 0))),
        out_shape=(jax.ShapeDtypeStruct(x.shape, F32),
                   jax.ShapeDtypeStruct((nb, CONV_W - 1, 2 * D_FF), F32)),
        scratch_shapes=[pltpu.VMEM((8, 2 * D_FF), F32),
                        pltpu.VMEM((T_FFN, D_FF), BF16)],
        compiler_params=pltpu.CompilerParams(dimension_semantics=("arbitrary", "arbitrary"),
                                             vmem_limit_bytes=VMEM_LIMIT),
        name="prompt_ffn",
    )(x, *weights)


def _sample_pre_kernel(wsm_ref, bsm_ref, x_ref, lbp_ref, gpre_ref, win_ref, lng_ref, lnb_ref, wpb_ref,
                       qi_ref, kh_ref, d_ref, i_ref, oin_ref, og_ref, gas_ref, hb_ref, vn_ref):
    n_rows = x_ref.shape[0]
    seq = 4
    x = x_ref[...]
    xn = _rms(x, gpre_ref[...]).astype(BF16)
    lb = _lower_bound(lbp_ref[...])

    def proj(off, width):
        return jnp.dot(xn, win_ref[:, off:off + width], preferred_element_type=F32)

    def tok(width):
        return _mod(_iota((n_rows, width), 0), seq)

    def shift(v, k):
        return pltpu.roll(v, k, 0) if k else v

    qf, logf, kk = _hgrn_features(proj(OFF_Q, D_QA), proj(OFF_F, D_QA), lb)
    iv = proj(OFF_I, D_VA)
    tq = tok(D_QA)
    b = logf + jnp.where(tq >= 1, shift(logf, 1), 0.0)
    b = b + jnp.where(tq >= 2, shift(b, 2), 0.0)
    b_last = jnp.where(tq == seq - 1, b, 0.0)
    b_last = b_last + pltpu.roll(b_last, n_rows - 1, 0)
    b_last = b_last + pltpu.roll(b_last, n_rows - 2, 0)
    qi_ref[...] = qf * jnp.exp(b)
    kh_ref[...] = kk * jnp.exp(b_last - b)
    d_ref[...] = jnp.exp(b_last)
    i_ref[...] = iv

    head_sum = _ones_where(_div(_iota((D_QA, D_VA), 0), DK_A) == _div(_iota((D_QA, D_VA), 1), DV_A))
    tv = tok(D_VA)
    o = jnp.zeros((n_rows, D_VA), F32)
    for k in range(seq):
        pair = qf * shift(kk, k) * jnp.exp(b - shift(b, k))
        sc = jnp.dot(jnp.where(tq >= k, pair, 0.0).astype(BF16), head_sum, preferred_element_type=F32)
        o = o + jnp.where(tv >= k, sc * shift(iv, k), 0.0)
    oin_ref[...] = o

    gu = _gelu(proj(OFF_U, D_B))
    vn = _layer_norm(_gelu(proj(OFF_V, D_B)), lng_ref[...], lnb_ref[...])
    vn_ref[...] = vn
    tg = tok(CH_B)
    obs = []
    for g in range(H_B):
        vg = vn[:, g * CH_B:(g + 1) * CH_B]
        s = jnp.zeros((n_rows, CH_B), F32)
        for t in range(seq):
            s = s + jnp.where(tg == t, bsm_ref[g * seq + t], 0.0)
        for k in range(seq):
            coef = jnp.zeros((n_rows, CH_B), F32)
            for t in range(k, seq):
                coef = coef + jnp.where(tg == t, wsm_ref[(g * seq + t) * seq + (t - k)], 0.0)
            s = s + coef * shift(vg, k)
        obs.append(gu[:, g * CH_B:(g + 1) * CH_B] * s)
    ob = jnp.concatenate(obs, axis=1)
    hb_ref[...] = jax.nn.sigmoid(proj(OFF_GB, D_MODEL)) * _dot(ob, wpb_ref[...])
    gas_ref[...] = jax.nn.sigmoid(proj(OFF_GA, D_MODEL))
    og_ref[...] = proj(OFF_OG, D_VA)


def _sample_pre(wsm, bsm, x, lbp, gpre, win, lng, lnb, wpb):
    n_rows = x.shape[0]
    shapes = [(n_rows, D_QA), (n_rows, D_QA), (n_rows, D_QA), (n_rows, D_VA), (n_rows, D_VA),
              (n_rows, D_VA), (n_rows, D_MODEL), (n_rows, D_MODEL), (n_rows, D_B)]
    smem = pl.BlockSpec(memory_space=pltpu.SMEM)
    return pl.pallas_call(
        _sample_pre_kernel,
        in_specs=[smem, smem] + [pl.BlockSpec(memory_space=pltpu.VMEM)] * 7,
        out_shape=tuple(jax.ShapeDtypeStruct(s, F32) for s in shapes),
        compiler_params=pltpu.CompilerParams(vmem_limit_bytes=VMEM_LIMIT),
        name="sample_pre",
    )(wsm, bsm, x, lbp, gpre, win, lng, lnb, wpb)


def _sample_state_kernel(qi_ref, kh_ref, d_ref, i_ref, s_ref, o_ref, sn_ref):
    seq = 4
    nseq = 4
    blk = nseq * seq
    rows_seq = _div(_iota((blk, DV_A), 0), seq)

    @pl.loop(0, qi_ref.shape[0] // blk)
    def _(m):
        row0 = pl.multiple_of(m * blk, blk)
        rows = pl.ds(row0, blk)
        outs = []
        for h in range(H_A):
            kcols = slice(h * DK_A, (h + 1) * DK_A)
            vcols = slice(h * DV_A, (h + 1) * DV_A)
            idx = [(m * nseq + bb) * H_A + h for bb in range(nseq)]
            st = [s_ref[i] for i in idx]
            q16 = qi_ref[rows, kcols].astype(BF16)
            r = lax.dot_general(q16, jnp.concatenate(st, axis=0).astype(BF16), (((1,), (1,)), ((), ())),
                                preferred_element_type=F32)
            oh = r[:, :DV_A]
            for bb in range(1, nseq):
                oh = jnp.where(rows_seq == bb, r[:, bb * DV_A:(bb + 1) * DV_A], oh)
            outs.append(oh)
            i16 = i_ref[rows, vcols]
            i_sel = jnp.concatenate([jnp.where(rows_seq == bb, i16, 0.0) for bb in range(nseq)], axis=1)
            inc = _dot_tn(i_sel, kh_ref[rows, kcols])
            d16 = d_ref[rows, kcols]
            for bb in range(nseq):
                sn_ref[idx[bb]] = st[bb] * d16[bb * seq:bb * seq + 1] + inc[bb * DV_A:(bb + 1) * DV_A]
        o_ref[rows, :] = jnp.concatenate(outs, axis=1)


def _sample_state(qi, kh, d, iv, state):
    n_rows = qi.shape[0]
    rb = SAMPLE_BB * 4
    sb = SAMPLE_BB * H_A
    return pl.pallas_call(
        _sample_state_kernel,
        grid=(n_rows // rb,),
        in_specs=[pl.BlockSpec((rb, D_QA), lambda i: (i, 0)),
                  pl.BlockSpec((rb, D_QA), lambda i: (i, 0)),
                  pl.BlockSpec((rb, D_QA), lambda i: (i, 0)),
                  pl.BlockSpec((rb, D_VA), lambda i: (i, 0)),
                  pl.BlockSpec((sb, DV_A, DK_A), lambda i: (i, 0, 0))],
        out_specs=(pl.BlockSpec((rb, D_VA), lambda i: (i, 0)),
                   pl.BlockSpec((sb, DV_A, DK_A), lambda i: (i, 0, 0))),
        out_shape=(jax.ShapeDtypeStruct((n_rows, D_VA), F32),
                   jax.ShapeDtypeStruct(state.shape, F32)),
        compiler_params=pltpu.CompilerParams(dimension_semantics=("arbitrary",),
                                             vmem_limit_bytes=VMEM_LIMIT),
        name="sample_state",
    )(qi, kh, d, iv, state)


def _sample_post_kernel(x_ref, oin_ref, oint_ref, og_ref, gas_ref, hb_ref, hng_ref, wpa_ref, wo_ref,
                        gpost_ref, y_ref):
    y_ref[...] = _mix_out(x_ref[...], oin_ref[...] + oint_ref[...], og_ref[...], gas_ref[...], hb_ref[...],
                          hng_ref[...], wpa_ref, wo_ref, gpost_ref[...])


def _sample_post(x, oin, oint, og, gas, hb, hng, wpa, wo, gpost):
    return pl.pallas_call(
        _sample_post_kernel,
        out_shape=jax.ShapeDtypeStruct(x.shape, F32),
        compiler_params=pltpu.CompilerParams(vmem_limit_bytes=VMEM_LIMIT),
        name="sample_post",
    )(x, oin, oint, og, gas, hb, hng, wpa, wo, gpost)


def _sample_ffn_kernel(x_ref, gpre_ref, wu_ref, cache_ref, cw_ref, cb_ref, wdn_ref, gpost_ref, y_ref, cs_ref,
                       xn_ref, acc_ref, old_ref, up_ref, val_ref):
    step = pl.program_id(0)
    nblk = pl.num_programs(0) // 2
    n_rows = x_ref.shape[0]
    n_seq = cache_ref.shape[0]
    seq = n_rows // n_seq
    keep = CONV_W - 1

    @pl.when(step == 0)
    def _():
        xn_ref[...] = _rms(x_ref[...], gpre_ref[...]).astype(BF16)
        acc_ref[...] = jnp.zeros_like(acc_ref)
        old_ref[...] = jnp.zeros_like(old_ref)

    up = jnp.dot(xn_ref[...], wu_ref[...], preferred_element_type=F32)
    up_ref[...] = up
    for b in range(n_seq):
        old_ref[seq * b:seq * b + keep, :] = cache_ref[b]
        cs_ref[b] = up_ref[seq * (b + 1) - keep:seq * (b + 1), :]
    old = old_ref[...]
    tok = _mod(_iota(up.shape, 0), seq)
    m2 = jnp.where(tok >= 2, pltpu.roll(up, 2, 0), old)
    m1 = jnp.where(tok >= 1, pltpu.roll(up, 1, 0), pltpu.roll(old, n_rows - 1, 0))
    conv = cb_ref[...] + cw_ref[0:1, :] * m2 + cw_ref[1:2, :] * m1 + cw_ref[2:3, :] * up

    @pl.when(step < nblk)
    def _():
        val_ref[step] = conv

    @pl.when(step >= nblk)
    def _():
        a = _gelu_mul(conv, val_ref[step - nblk])
        acc_ref[...] += _dot(a, wdn_ref[...])

    @pl.when(step == pl.num_programs(0) - 1)
    def _():
        y_ref[...] = x_ref[...] + _rms(acc_ref[...], gpost_ref[...])


def _sample_ffn(x, gpre, wup, cache, cw, cb, wdn, gpost):
    n_rows = x.shape[0]
    n_seq = cache.shape[0]
    nblk = D_FF // SAMPLE_FFN_CB
    col = lambda i: (0, lax.rem(i + nblk, 2 * nblk))
    col3 = lambda i: (0, 0, lax.rem(i + nblk, 2 * nblk))
    full = lambda shape: pl.BlockSpec(shape, lambda i: (0,) * len(shape))
    return pl.pallas_call(
        _sample_ffn_kernel,
        grid=(2 * nblk,),
        in_specs=[full((n_rows, D_MODEL)), full((1, D_MODEL)),
                  pl.BlockSpec((D_MODEL, SAMPLE_FFN_CB), col),
                  pl.BlockSpec((n_seq, CONV_W - 1, SAMPLE_FFN_CB), col3),
                  pl.BlockSpec((CONV_W, SAMPLE_FFN_CB), col),
                  pl.BlockSpec((1, SAMPLE_FFN_CB), col),
                  pl.BlockSpec((SAMPLE_FFN_CB, D_MODEL), lambda i: (jnp.maximum(i - nblk, 0), 0)),
                  full((1, D_MODEL))],
        out_specs=(full((n_rows, D_MODEL)),
                   pl.BlockSpec((n_seq, CONV_W - 1, SAMPLE_FFN_CB), col3)),
        out_shape=(jax.ShapeDtypeStruct((n_rows, D_MODEL), F32),
                   jax.ShapeDtypeStruct(cache.shape, F32)),
        scratch_shapes=[pltpu.VMEM((n_rows, D_MODEL), BF16), pltpu.VMEM((n_rows, D_MODEL), F32),
                        pltpu.VMEM((n_rows, SAMPLE_FFN_CB), F32), pltpu.VMEM((n_rows, SAMPLE_FFN_CB), F32),
                        pltpu.VMEM((nblk, n_rows, SAMPLE_FFN_CB), F32)],
        compiler_params=pltpu.CompilerParams(dimension_semantics=("arbitrary",),
                                             vmem_limit_bytes=VMEM_LIMIT),
        name="sample_ffn",
    )(x, gpre, wup, cache, cw, cb, wdn, gpost)


def kernel(x_prompt, x_sample, state_hgrn, cache_ffn_conv, lb_param, mix_pre_g, w_in, hgrn_norm_g, gmlp_ln_g, gmlp_ln_b, w_s, b_s, w_pa, w_pb, w_o, mix_post_g, ffn_pre_g, w_up, conv_w, conv_b, w_down, ffn_post_g):
    nb_s, seq_s, _ = x_sample.shape
    row = lambda v: v.reshape(1, -1)
    win, wpa, wpb, wo = (w[0].astype(BF16) for w in (w_in, w_pa, w_pb, w_o))
    wup, wdn, ws = w_up[0].astype(BF16), w_down[0].astype(BF16), w_s[0].astype(BF16)
    gpre, gpost, fpre, fpost = row(mix_pre_g[0]), row(mix_post_g[0]), row(ffn_pre_g[0]), row(ffn_post_g[0])
    lng, lnb, cb = row(gmlp_ln_g[0]), row(gmlp_ln_b[0]), row(conv_b[0])
    hng = row(jnp.tile(hgrn_norm_g[0], H_A))
    bsb = jnp.broadcast_to(b_s[0][:, :, None], (H_B, GMLP_CHUNK, CH_B))

    x1, sp = _prompt_mixer(x_prompt, lb_param, gpre, win, hng, lng, lnb, ws, bsb, wpa, wpb, wo, gpost)
    yp, cp = _prompt_ffn(x1, fpre, wup, conv_w[0], cb, wdn, fpost)

    xs = x_sample.reshape(nb_s * seq_s, D_MODEL)
    wsm = w_s[0][:, :seq_s, :seq_s].reshape(-1)
    bsm = b_s[0][:, :seq_s].reshape(-1)
    qi, kh, d, iv, oin, og, gas, hb, vn = _sample_pre(wsm, bsm, xs, lb_param, gpre, win, lng, lnb, wpb)
    st_in = jnp.swapaxes(state_hgrn[0], -1, -2).reshape(nb_s * H_A, DV_A, DK_A)
    oint, ss = _sample_state(qi, kh, d, iv, st_in)
    xs1 = _sample_post(xs, oin, oint, og, gas, hb, hng, wpa, wo, gpost)
    ys, cs = _sample_ffn(xs1, fpre, wup, cache_ffn_conv[0], conv_w[0], cb, wdn, fpost)

    return (yp, ys.reshape(nb_s, seq_s, D_MODEL), jnp.swapaxes(sp, -1, -2)[None],
            jnp.swapaxes(ss.reshape(1, nb_s, H_A, DV_A, DK_A), -1, -2), cp[None], cs[None],
            vn.reshape(1, nb_s, seq_s, D_B))
```

```python
import math

import jax
import jax.numpy as jnp
import numpy as np
from jax import lax
from jax.experimental import pallas as pl
from jax.experimental.pallas import tpu as pltpu

F32 = jnp.float32
BF16 = jnp.bfloat16

D_MODEL = 1024
H_A, DK_A, DV_A = 8, 128, 64
D_QA, D_VA = H_A * DK_A, H_A * DV_A
H_B, CH_B = 4, 128
D_B = H_B * CH_B
GMLP_CHUNK = 128
D_FF = 2816
CONV_W = 3
EPS = 1e-6

OFF_Q, OFF_F, OFF_I, OFF_OG, OFF_U, OFF_V, OFF_GA, OFF_GB = 0, 1024, 2048, 2560, 3072, 3584, 4096, 5120
D_IN = 6144

HGRN_C = 64
PAIR_K = 2 * DK_A
PAIR_V = 2 * DV_A
N_PAIR = H_A // 2
T_MIX = 512
T_FFN = 1024
FFN_CB = 256
SAMPLE_FFN_CB = 1408
SAMPLE_BB = 32
VMEM_LIMIT = 56 * 1024 * 1024


def _dot(a, b):
    return jnp.dot(a.astype(BF16), b.astype(BF16), preferred_element_type=F32)


def _dot_nt(a, b):
    return lax.dot_general(a.astype(BF16), b.astype(BF16), (((1,), (1,)), ((), ())),
                           preferred_element_type=F32)


def _dot_tn(a, b):
    return lax.dot_general(a.astype(BF16), b.astype(BF16), (((0,), (0,)), ((), ())),
                           preferred_element_type=F32)


def _dot_split(m, x):
    hi = x.astype(BF16)
    lo = (x - hi.astype(F32)).astype(BF16)
    m = m.astype(BF16)
    return (jnp.dot(m, hi, preferred_element_type=F32) + jnp.dot(m, lo, preferred_element_type=F32))


def _rms(x, g):
    return x * lax.rsqrt(jnp.mean(x * x, axis=-1, keepdims=True) + EPS) * g


_GELU_K1 = -2.0 * math.sqrt(2.0 / math.pi) * math.log2(math.e)
_GELU_K3 = _GELU_K1 * 0.044715


def _bf16_split(v):
    hi = float(np.asarray(v, np.float32).astype(jnp.bfloat16))
    return hi, float(np.asarray(v - hi, np.float32).astype(jnp.bfloat16))


def _gelu_gate(x):
    s = x * x
    if x.dtype == F32:
        poly = _GELU_K1 + _GELU_K3 * s
    else:
        (k1h, k1l), (k3h, k3l) = _bf16_split(_GELU_K1), _bf16_split(_GELU_K3)
        poly = (k1h + k3h * s) + (k1l + k3l * s)
    return 1.0 / (1.0 + jnp.exp2(x * poly))


def _gelu(x):
    return x * _gelu_gate(x)


def _gelu_mul(x, v):
    return (x * v) * _gelu_gate(x)


def _layer_norm(x, g, b):
    xc = x - jnp.mean(x, axis=-1, keepdims=True)
    return xc * lax.rsqrt(jnp.mean(xc * xc, axis=-1, keepdims=True) + EPS) * g + b


def _lower_bound(lbp):
    e = jnp.exp(lbp - jnp.max(lbp, axis=0, keepdims=True))
    return e[0:1] / jnp.sum(e, axis=0, keepdims=True)


def _hgrn_features(q, f_logit, lb):
    f = lb + (1.0 - lb) * jax.nn.sigmoid(f_logit)
    return q * jax.nn.sigmoid(q), jnp.log(f), 1.0 - f


def _iota(shape, dim):
    return lax.broadcasted_iota(jnp.int32, shape, dim)


def _div(x, n):
    assert n & (n - 1) == 0
    return lax.shift_right_logical(x, n.bit_length() - 1)


def _mod(x, n):
    assert n & (n - 1) == 0
    return x & (n - 1)


def _ones_where(cond):
    return jnp.where(cond, 1.0, 0.0).astype(BF16)


def _head_mean_square(o):
    same_head = _ones_where(_div(_iota((D_VA, D_VA), 0), DV_A) == _div(_iota((D_VA, D_VA), 1), DV_A))
    return _dot(o * o, same_head) * (1.0 / DV_A)


def _mix_out(x, o, og, ga_sig, hb, hng, wpa_ref, wo_ref, gpost):
    oa = o * lax.rsqrt(_head_mean_square(o) + EPS) * hng * (og * jax.nn.sigmoid(og))
    h = ga_sig * _dot(oa, wpa_ref[...]) + hb
    return x + _rms(_dot(h, wo_ref[...]), gpost)


def _prompt_mixer_kernel(x_ref, lbp_ref, gpre_ref, win_ref, hng_ref, lng_ref, lnb_ref, ws_ref, bsb_ref,
                         wpa_ref, wpb_ref, wo_ref, gpost_ref, y_ref, sp_ref,
                         st_ref, qt_ref, kt_ref, qi_ref, kh_ref, iv_ref, d_ref, sc_ref, upd_ref, stb_ref,
                         o_ref, ob_ref):
    j = pl.program_id(1)
    t_tile = x_ref.shape[1]
    c = HGRN_C
    n_chunk = t_tile // c

    @pl.when(j == 0)
    def _():
        st_ref[...] = jnp.zeros_like(st_ref)

    x = x_ref[0]
    xn = _rms(x, gpre_ref[...]).astype(BF16)
    lb = _lower_bound(lbp_ref[...])

    def proj(off, width):
        return jnp.dot(xn, win_ref[:, off:off + width], preferred_element_type=F32)

    q_raw, f_raw = proj(OFF_Q, D_QA), proj(OFF_F, D_QA)
    iv_ref[...] = proj(OFF_I, D_VA).astype(BF16)
    u_raw, v_raw, og_raw = proj(OFF_U, D_B), proj(OFF_V, D_B), proj(OFF_OG, D_VA)
    ga_raw, gb_raw = proj(OFF_GA, D_MODEL), proj(OFF_GB, D_MODEL)
    qf, logf, kk = _hgrn_features(q_raw, f_raw, lb)
    r, s = _iota((c + 8, c), 0), _iota((c + 8, c), 1)
    cum_m = jnp.where(r < c, jnp.where(s <= r, 1.0, 0.0) - jnp.where(s < c // 2, 1.0, 0.0), 1.0)
    for ci in range(n_chunk):
        rows = slice(ci * c, (ci + 1) * c)
        lf = logf[rows]
        cum = _dot_split(cum_m, lf)
        a = cum[:c]
        b_last = cum[c:c + 1]
        b_mid = lf[0:1] - a[0:1]
        qt = qf[rows] * jnp.exp(a)
        kt = kk[rows] * jnp.exp(-a)
        qt_ref[rows, :] = qt.astype(BF16)
        kt_ref[rows, :] = kt.astype(BF16)
        qi_ref[rows, :] = (qt * jnp.exp(b_mid)).astype(BF16)
        kh_ref[rows, :] = (kt * jnp.exp(b_last - b_mid)).astype(BF16)
        d_ref[ci:ci + 1, :] = jnp.exp(b_last)

    kbd_mask = _div(_iota((2 * c, PAIR_K), 0), c) == _div(_iota((2 * c, PAIR_K), 1), DK_A)
    ibd4_mask = _div(_iota((4 * c, 2 * PAIR_V), 0), c) == _div(_iota((4 * c, 2 * PAIR_V), 1), DV_A)
    sbd_mask = _div(_iota((PAIR_V, PAIR_K), 0), DV_A) == _div(_iota((PAIR_V, PAIR_K), 1), DK_A)
    causal = _mod(_iota((c, 2 * c), 1), c) <= _iota((c, 2 * c), 0)
    zero = jnp.zeros((), BF16)

    for p in range(N_PAIR):
        kcols = slice(p * PAIR_K, (p + 1) * PAIR_K)
        vcols = slice(p * PAIR_V, (p + 1) * PAIR_V)
        for ci in range(n_chunk):
            rows = slice(ci * c, (ci + 1) * c)
            kt = kt_ref[rows, kcols]
            kbd = jnp.where(kbd_mask, jnp.concatenate([kt, kt], axis=0), zero)
            sc = lax.dot_general(qt_ref[rows, kcols], kbd, (((1,), (1,)), ((), ())), preferred_element_type=F32)
            sc_ref[rows, vcols] = jnp.where(causal, sc, 0.0).astype(BF16)
            inc = lax.dot_general(iv_ref[rows, vcols], kh_ref[rows, kcols], (((0,), (0,)), ((), ())),
                                  preferred_element_type=F32)
            upd_ref[p * n_chunk + ci] = jnp.where(sbd_mask, inc, 0.0)

    for p in range(N_PAIR):
        kcols = slice(p * PAIR_K, (p + 1) * PAIR_K)
        st = st_ref[p]
        for ci in range(n_chunk):
            stb_ref[p * n_chunk + ci] = st.astype(BF16)
            st = st * d_ref[ci:ci + 1, kcols] + upd_ref[p * n_chunk + ci]
        st_ref[p] = st

    for q4 in range(N_PAIR // 2):
        vcols4 = slice(2 * q4 * PAIR_V, 2 * (q4 + 1) * PAIR_V)
        for ci in range(n_chunk):
            rows = slice(ci * c, (ci + 1) * c)
            ic4 = iv_ref[rows, vcols4]
            ibd4 = jnp.where(ibd4_mask, jnp.concatenate([ic4] * 4, axis=0), zero)
            inter = [lax.dot_general(qi_ref[rows, p * PAIR_K:(p + 1) * PAIR_K], stb_ref[p * n_chunk + ci],
                                     (((1,), (1,)), ((), ())), preferred_element_type=F32)
                     for p in (2 * q4, 2 * q4 + 1)]
            o_ref[rows, vcols4] = (jnp.dot(sc_ref[rows, vcols4], ibd4, preferred_element_type=F32)
                                   + jnp.concatenate(inter, axis=1))

    gu = _gelu(u_raw)
    vn = _layer_norm(_gelu(v_raw), lng_ref[...], lnb_ref[...]).astype(BF16)
    tril = _iota((GMLP_CHUNK, GMLP_CHUNK), 1) <= _iota((GMLP_CHUNK, GMLP_CHUNK), 0)
    zblk = jnp.zeros((GMLP_CHUNK, CH_B), BF16)
    for g2 in range(H_B // 2):
        g0, g1 = 2 * g2, 2 * g2 + 1
        w2 = jnp.concatenate([jnp.where(tril, ws_ref[g0], zero), jnp.where(tril, ws_ref[g1], zero)], axis=1)
        b2 = jnp.concatenate([bsb_ref[g0], bsb_ref[g1]], axis=1)
        cols2 = slice(g0 * CH_B, (g1 + 1) * CH_B)
        for n in range(t_tile // GMLP_CHUNK):
            rows = slice(n * GMLP_CHUNK, (n + 1) * GMLP_CHUNK)
            v2 = vn[rows, cols2]
            vbd = jnp.concatenate([jnp.concatenate([v2[:, :CH_B], zblk], axis=1),
                                   jnp.concatenate([zblk, v2[:, CH_B:]], axis=1)], axis=0)
            ob_ref[rows, cols2] = gu[rows, cols2] * (jnp.dot(w2, vbd, preferred_element_type=F32) + b2)

    hb = jax.nn.sigmoid(gb_raw) * _dot(ob_ref[...], wpb_ref[...])
    y_ref[0] = _mix_out(x, o_ref[...], og_raw, jax.nn.sigmoid(ga_raw), hb,
                        hng_ref[...], wpa_ref, wo_ref, gpost_ref[...])

    @pl.when(j == pl.num_programs(1) - 1)
    def _():
        for h in range(H_A):
            hh = h % 2
            sp_ref[0, h] = st_ref[h // 2, hh * DV_A:(hh + 1) * DV_A, hh * DK_A:(hh + 1) * DK_A]


def _const_spec(shape):
    return pl.BlockSpec(shape, lambda *_: (0,) * len(shape), pipeline_mode=pl.Buffered(1))


def _prompt_mixer(x, lbp, gpre, win, hng, lng, lnb, ws, bsb, wpa, wpb, wo, gpost):
    nb, seq, _ = x.shape
    grid = (nb, seq // T_MIX)
    n_pc = N_PAIR * (T_MIX // HGRN_C)
    weights = (lbp, gpre, win, hng, lng, lnb, ws, bsb, wpa, wpb, wo, gpost)
    return pl.pallas_call(
        _prompt_mixer_kernel,
        grid=grid,
        in_specs=[pl.BlockSpec((1, T_MIX, D_MODEL), lambda b, j: (b, j, 0))]
        + [_const_spec(w.shape) for w in weights],
        out_specs=(pl.BlockSpec((1, T_MIX, D_MODEL), lambda b, j: (b, j, 0)),
                   pl.BlockSpec((1, H_A, DV_A, DK_A), lambda b, j: (b, 0, 0, 0))),
        out_shape=(jax.ShapeDtypeStruct(x.shape, F32),
                   jax.ShapeDtypeStruct((nb, H_A, DV_A, DK_A), F32)),
        scratch_shapes=[pltpu.VMEM((N_PAIR, PAIR_V, PAIR_K), F32),
                        pltpu.VMEM((T_MIX, D_QA), BF16),
                        pltpu.VMEM((T_MIX, D_QA), BF16),
                        pltpu.VMEM((T_MIX, D_QA), BF16),
                        pltpu.VMEM((T_MIX, D_QA), BF16),
                        pltpu.VMEM((T_MIX, D_VA), BF16),
                        pltpu.VMEM((T_MIX // HGRN_C, D_QA), F32),
                        pltpu.VMEM((T_MIX, D_VA), BF16),
                        pltpu.VMEM((n_pc, PAIR_V, PAIR_K), F32),
                        pltpu.VMEM((n_pc, PAIR_V, PAIR_K), BF16),
                        pltpu.VMEM((T_MIX, D_VA), F32),
                        pltpu.VMEM((T_MIX, D_B), F32)],
        compiler_params=pltpu.CompilerParams(dimension_semantics=("arbitrary", "arbitrary"),
                                             vmem_limit_bytes=VMEM_LIMIT),
        name="prompt_mixer",
    )(x, *weights)


def _prompt_ffn_kernel(x_ref, gpre_ref, wup_ref, cw_ref, cb_ref, wdn_ref, gpost_ref, y_ref, cp_ref,
                       tail_ref, a_ref):
    j = pl.program_id(1)
    t_tile = x_ref.shape[1]

    @pl.when(j == 0)
    def _():
        tail_ref[...] = jnp.zeros_like(tail_ref)

    x = x_ref[0]
    xn = _rms(x, gpre_ref[...]).astype(BF16)
    row = _iota((t_tile, FFN_CB), 0)

    def up_proj(c0):
        return jnp.dot(xn, wup_ref[:, c0:c0 + FFN_CB], preferred_element_type=F32)

    def conv(c0, up):
        cols = slice(c0, c0 + FFN_CB)
        p0 = tail_ref[6:7, cols]
        p1 = tail_ref[7:8, cols]
        tail_ref[:, cols] = up[t_tile - 8:]
        m1 = jnp.where(row == 0, p1, pltpu.roll(up, 1, 0))
        m2 = jnp.where(row == 0, p0, jnp.where(row == 1, p1, pltpu.roll(up, 2, 0)))
        w0, w1, w2 = (cw_ref[k:k + 1, cols].astype(BF16) for k in range(CONV_W))
        return (cb_ref[:, cols].astype(BF16) + w0 * m2.astype(BF16) + w1 * m1.astype(BF16)
                + w2 * up.astype(BF16))

    nblk = D_FF // FFN_CB
    ups = (up_proj(0), up_proj(D_FF))
    for blk in range(nblk):
        c0 = blk * FFN_CB
        nxt = (up_proj(c0 + FFN_CB), up_proj(D_FF + c0 + FFN_CB)) if blk + 1 < nblk else None
        a_ref[:, c0:c0 + FFN_CB] = _gelu_mul(conv(c0, ups[0]), conv(D_FF + c0, ups[1])).astype(BF16)
        ups = nxt

    y_ref[0] = x + _rms(jnp.dot(a_ref[...], wdn_ref[...], preferred_element_type=F32), gpost_ref[...])

    @pl.when(j == pl.num_programs(1) - 1)
    def _():
        cp_ref[0] = tail_ref[6:8, :]


def _prompt_ffn(x, gpre, wup, cw, cb, wdn, gpost):
    nb, seq, _ = x.shape
    weights = (gpre, wup, cw, cb, wdn, gpost)
    return pl.pallas_call(
        _prompt_ffn_kernel,
        grid=(nb, seq // T_FFN),
        in_specs=[pl.BlockSpec((1, T_FFN, D_MODEL), lambda b, j: (b, j, 0))]
        + [_const_spec(w.shape) for w in weights],
        out_specs=(pl.BlockSpec((1, T_FFN, D_MODEL), lambda b, j: (b, j, 0)),
                   pl.BlockSpec((1, CONV_W - 1, 2 * D_FF), lambda b, j: (b, 0, 0))),
        out_shape=(jax.ShapeDtypeStruct(x.shape, F32),
                   jax.ShapeDtypeStruct((nb, CONV_W - 1, 2 * D_FF), F32)),
        scratch_shapes=[pltpu.VMEM((8, 2 * D_FF), F32),
                        pltpu.VMEM((T_FFN, D_FF), BF16)],
        compiler_params=pltpu.CompilerParams(dimension_semantics=("arbitrary", "arbitrary"),
                                             vmem_limit_bytes=VMEM_LIMIT),
        name="prompt_ffn",
    )(x, *weights)


def _sample_pre_kernel(wsm_ref, bsm_ref, x_ref, lbp_ref, gpre_ref, win_ref, lng_ref, lnb_ref, wpb_ref,
                       qi_ref, kh_ref, d_ref, i_ref, oin_ref, og_ref, gas_ref, hb_ref, vn_ref):
    n_rows = x_ref.shape[0]
    seq = 4
    x = x_ref[...]
    xn = _rms(x, gpre_ref[...]).astype(BF16)
    lb = _lower_bound(lbp_ref[...])

    def proj(off, width):
        return jnp.dot(xn, win_ref[:, off:off + width], preferred_element_type=F32)

    def tok(width):
        return _mod(_iota((n_rows, width), 0), seq)

    def shift(v, k):
        return pltpu.roll(v, k, 0) if k else v

    qf, logf, kk = _hgrn_features(proj(OFF_Q, D_QA), proj(OFF_F, D_QA), lb)
    iv = proj(OFF_I, D_VA)
    tq = tok(D_QA)
    b = logf + jnp.where(tq >= 1, shift(logf, 1), 0.0)
    b = b + jnp.where(tq >= 2, shift(b, 2), 0.0)
    b_last = jnp.where(tq == seq - 1, b, 0.0)
    b_last = b_last + pltpu.roll(b_last, n_rows - 1, 0)
    b_last = b_last + pltpu.roll(b_last, n_rows - 2, 0)
    qi_ref[...] = qf * jnp.exp(b)
    kh_ref[...] = kk * jnp.exp(b_last - b)
    d_ref[...] = jnp.exp(b_last)
    i_ref[...] = iv

    head_sum = _ones_where(_div(_iota((D_QA, D_VA), 0), DK_A) == _div(_iota((D_QA, D_VA), 1), DV_A))
    tv = tok(D_VA)
    o = jnp.zeros((n_rows, D_VA), F32)
    for k in range(seq):
        pair = qf * shift(kk, k) * jnp.exp(b - shift(b, k))
        sc = jnp.dot(jnp.where(tq >= k, pair, 0.0).astype(BF16), head_sum, preferred_element_type=F32)
        o = o + jnp.where(tv >= k, sc * shift(iv, k), 0.0)
    oin_ref[...] = o

    gu = _gelu(proj(OFF_U, D_B))
    vn = _layer_norm(_gelu(proj(OFF_V, D_B)), lng_ref[...], lnb_ref[...])
    vn_ref[...] = vn
    tg = tok(CH_B)
    obs = []
    for g in range(H_B):
        vg = vn[:, g * CH_B:(g + 1) * CH_B]
        s = jnp.zeros((n_rows, CH_B), F32)
        for t in range(seq):
            s = s + jnp.where(tg == t, bsm_ref[g * seq + t], 0.0)
        for k in range(seq):
            coef = jnp.zeros((n_rows, CH_B), F32)
            for t in range(k, seq):
                coef = coef + jnp.where(tg == t, wsm_ref[(g * seq + t) * seq + (t - k)], 0.0)
            s = s + coef * shift(vg, k)
        obs.append(gu[:, g * CH_B:(g + 1) * CH_B] * s)
    ob = jnp.concatenate(obs, axis=1)
    hb_ref[...] = jax.nn.sigmoid(proj(OFF_GB, D_MODEL)) * _dot(ob, wpb_ref[...])
    gas_ref[...] = jax.nn.sigmoid(proj(OFF_GA, D_MODEL))
    og_ref[...] = proj(OFF_OG, D_VA)


def _sample_pre(wsm, bsm, x, lbp, gpre, win, lng, lnb, wpb):
    n_rows = x.shape[0]
    shapes = [(n_rows, D_QA), (n_rows, D_QA), (n_rows, D_QA), (n_rows, D_VA), (n_rows, D_VA),
              (n_rows, D_VA), (n_rows, D_MODEL), (n_rows, D_MODEL), (n_rows, D_B)]
    smem = pl.BlockSpec(memory_space=pltpu.SMEM)
    return pl.pallas_call(
        _sample_pre_kernel,
        in_specs=[smem, smem] + [pl.BlockSpec(memory_space=pltpu.VMEM)] * 7,
        out_shape=tuple(jax.ShapeDtypeStruct(s, F32) for s in shapes),
        compiler_params=pltpu.CompilerParams(vmem_limit_bytes=VMEM_LIMIT),
        name="sample_pre",
    )(wsm, bsm, x, lbp, gpre, win, lng, lnb, wpb)


def _sample_state_kernel(qi_ref, kh_ref, d_ref, i_ref, s_ref, o_ref, sn_ref):
    seq = 4
    nseq = 4
    blk = nseq * seq
    rows_seq = _div(_iota((blk, DV_A), 0), seq)

    @pl.loop(0, qi_ref.shape[0] // blk)
    def _(m):
        row0 = pl.multiple_of(m * blk, blk)
        rows = pl.ds(row0, blk)
        outs = []
        for h in range(H_A):
            kcols = slice(h * DK_A, (h + 1) * DK_A)
            vcols = slice(h * DV_A, (h + 1) * DV_A)
            idx = [(m * nseq + bb) * H_A + h for bb in range(nseq)]
            st = [s_ref[i] for i in idx]
            q16 = qi_ref[rows, kcols].astype(BF16)
            r = lax.dot_general(q16, jnp.concatenate(st, axis=0).astype(BF16), (((1,), (1,)), ((), ())),
                                preferred_element_type=F32)
            oh = r[:, :DV_A]
            for bb in range(1, nseq):
                oh = jnp.where(rows_seq == bb, r[:, bb * DV_A:(bb + 1) * DV_A], oh)
            outs.append(oh)
            i16 = i_ref[rows, vcols]
            i_sel = jnp.concatenate([jnp.where(rows_seq == bb, i16, 0.0) for bb in range(nseq)], axis=1)
            inc = _dot_tn(i_sel, kh_ref[rows, kcols])
            d16 = d_ref[rows, kcols]
            for bb in range(nseq):
                sn_ref[idx[bb]] = st[bb] * d16[bb * seq:bb * seq + 1] + inc[bb * DV_A:(bb + 1) * DV_A]
        o_ref[rows, :] = jnp.concatenate(outs, axis=1)


def _sample_state(qi, kh, d, iv, state):
    n_rows = qi.shape[0]
    rb = SAMPLE_BB * 4
    sb = SAMPLE_BB * H_A
    return pl.pallas_call(
        _sample_state_kernel,
        grid=(n_rows // rb,),
        in_specs=[pl.BlockSpec((rb, D_QA), lambda i: (i, 0)),
                  pl.BlockSpec((rb, D_QA), lambda i: (i, 0)),
                  pl.BlockSpec((rb, D_QA), lambda i: (i, 0)),
                  pl.BlockSpec((rb, D_VA), lambda i: (i, 0)),
                  pl.BlockSpec((sb, DV_A, DK_A), lambda i: (i, 0, 0))],
        out_specs=(pl.BlockSpec((rb, D_VA), lambda i: (i, 0)),
                   pl.BlockSpec((sb, DV_A, DK_A), lambda i: (i, 0, 0))),
        out_shape=(jax.ShapeDtypeStruct((n_rows, D_VA), F32),
                   jax.ShapeDtypeStruct(state.shape, F32)),
        compiler_params=pltpu.CompilerParams(dimension_semantics=("arbitrary",),
                                             vmem_limit_bytes=VMEM_LIMIT),
        name="sample_state",
    )(qi, kh, d, iv, state)


def _sample_post_kernel(x_ref, oin_ref, oint_ref, og_ref, gas_ref, hb_ref, hng_ref, wpa_ref, wo_ref,
                        gpost_ref, y_ref):
    y_ref[...] = _mix_out(x_ref[...], oin_ref[...] + oint_ref[...], og_ref[...], gas_ref[...], hb_ref[...],
                          hng_ref[...], wpa_ref, wo_ref, gpost_ref[...])


def _sample_post(x, oin, oint, og, gas, hb, hng, wpa, wo, gpost):
    return pl.pallas_call(
        _sample_post_kernel,
        out_shape=jax.ShapeDtypeStruct(x.shape, F32),
        compiler_params=pltpu.CompilerParams(vmem_limit_bytes=VMEM_LIMIT),
        name="sample_post",
    )(x, oin, oint, og, gas, hb, hng, wpa, wo, gpost)


def _sample_ffn_kernel(x_ref, gpre_ref, wu_ref, cache_ref, cw_ref, cb_ref, wdn_ref, gpost_ref, y_ref, cs_ref,
                       xn_ref, acc_ref, old_ref, up_ref, val_ref):
    step = pl.program_id(0)
    nblk = pl.num_programs(0) // 2
    n_rows = x_ref.shape[0]
    n_seq = cache_ref.shape[0]
    seq = n_rows // n_seq
    keep = CONV_W - 1

    @pl.when(step == 0)
    def _():
        xn_ref[...] = _rms(x_ref[...], gpre_ref[...]).astype(BF16)
        acc_ref[...] = jnp.zeros_like(acc_ref)
        old_ref[...] = jnp.zeros_like(old_ref)

    up = jnp.dot(xn_ref[...], wu_ref[...], preferred_element_type=F32)
    up_ref[...] = up
    for b in range(n_seq):
        old_ref[seq * b:seq * b + keep, :] = cache_ref[b]
        cs_ref[b] = up_ref[seq * (b + 1) - keep:seq * (b + 1), :]
    old = old_ref[...]
    tok = _mod(_iota(up.shape, 0), seq)
    m2 = jnp.where(tok >= 2, pltpu.roll(up, 2, 0), old)
    m1 = jnp.where(tok >= 1, pltpu.roll(up, 1, 0), pltpu.roll(old, n_rows - 1, 0))
    conv = cb_ref[...] + cw_ref[0:1, :] * m2 + cw_ref[1:2, :] * m1 + cw_ref[2:3, :] * up

    @pl.when(step < nblk)
    def _():
        val_ref[step] = conv

    @pl.when(step >= nblk)
    def _():
        a = _gelu_mul(conv, val_ref[step - nblk])
        acc_ref[...] += _dot(a, wdn_ref[...])

    @pl.when(step == pl.num_programs(0) - 1)
    def _():
        y_ref[...] = x_ref[...] + _rms(acc_ref[...], gpost_ref[...])


def _sample_ffn(x, gpre, wup, cache, cw, cb, wdn, gpost):
    n_rows = x.shape[0]
    n_seq = cache.shape[0]
    nblk = D_FF // SAMPLE_FFN_CB
    col = lambda i: (0, lax.rem(i + nblk, 2 * nblk))
    col3 = lambda i: (0, 0, lax.rem(i + nblk, 2 * nblk))
    full = lambda shape: pl.BlockSpec(shape, lambda i: (0,) * len(shape))
    return pl.pallas_call(
        _sample_ffn_kernel,
        grid=(2 * nblk,),
        in_specs=[full((n_rows, D_MODEL)), full((1, D_MODEL)),
                  pl.BlockSpec((D_MODEL, SAMPLE_FFN_CB), col),
                  pl.BlockSpec((n_seq, CONV_W - 1, SAMPLE_FFN_CB), col3),
                  pl.BlockSpec((CONV_W, SAMPLE_FFN_CB), col),
                  pl.BlockSpec((1, SAMPLE_FFN_CB), col),
                  pl.BlockSpec((SAMPLE_FFN_CB, D_MODEL), lambda i: (jnp.maximum(i - nblk, 0), 0)),
                  full((1, D_MODEL))],
        out_specs=(full((n_rows, D_MODEL)),
                   pl.BlockSpec((n_seq, CONV_W - 1, SAMPLE_FFN_CB), col3)),
        out_shape=(jax.ShapeDtypeStruct((n_rows, D_MODEL), F32),
                   jax.ShapeDtypeStruct(cache.shape, F32)),
        scratch_shapes=[pltpu.VMEM((n_rows, D_MODEL), BF16), pltpu.VMEM((n_rows, D_MODEL), F32),
                        pltpu.VMEM((n_rows, SAMPLE_FFN_CB), F32), pltpu.VMEM((n_rows, SAMPLE_FFN_CB), F32),
                        pltpu.VMEM((nblk, n_rows, SAMPLE_FFN_CB), F32)],
        compiler_params=pltpu.CompilerParams(dimension_semantics=("arbitrary",),
                                             vmem_limit_bytes=VMEM_LIMIT),
        name="sample_ffn",
    )(x, gpre, wup, cache, cw, cb, wdn, gpost)


def kernel(x_prompt, x_sample, state_hgrn, cache_ffn_conv, lb_param, mix_pre_g, w_in, hgrn_norm_g, gmlp_ln_g, gmlp_ln_b, w_s, b_s, w_pa, w_pb, w_o, mix_post_g, ffn_pre_g, w_up, conv_w, conv_b, w_down, ffn_post_g):
    nb_s, seq_s, _ = x_sample.shape
    row = lambda v: v.reshape(1, -1)
    win, wpa, wpb, wo = (w[0].astype(BF16) for w in (w_in, w_pa, w_pb, w_o))
    wup, wdn, ws = w_up[0].astype(BF16), w_down[0].astype(BF16), w_s[0].astype(BF16)
    gpre, gpost, fpre, fpost = row(mix_pre_g[0]), row(mix_post_g[0]), row(ffn_pre_g[0]), row(ffn_post_g[0])
    lng, lnb, cb = row(gmlp_ln_g[0]), row(gmlp_ln_b[0]), row(conv_b[0])
    hng = row(jnp.tile(hgrn_norm_g[0], H_A))
    bsb = jnp.broadcast_to(b_s[0][:, :, None], (H_B, GMLP_CHUNK, CH_B))

    x1, sp = _prompt_mixer(x_prompt, lb_param, gpre, win, hng, lng, lnb, ws, bsb, wpa, wpb, wo, gpost)
    yp, cp = _prompt_ffn(x1, fpre, wup, conv_w[0], cb, wdn, fpost)

    xs = x_sample.reshape(nb_s * seq_s, D_MODEL)
    wsm = w_s[0][:, :seq_s, :seq_s].reshape(-1)
    bsm = b_s[0][:, :seq_s].reshape(-1)
    qi, kh, d, iv, oin, og, gas, hb, vn = _sample_pre(wsm, bsm, xs, lb_param, gpre, win, lng, lnb, wpb)
    st_in = jnp.swapaxes(state_hgrn[0], -1, -2).reshape(nb_s * H_A, DV_A, DK_A)
    oint, ss = _sample_state(qi, kh, d, iv, st_in)
    xs1 = _sample_post(xs, oin, oint, og, gas, hb, hng, wpa, wo, gpost)
    ys, cs = _sample_ffn(xs1, fpre, wup, cache_ffn_conv[0], conv_w[0], cb, wdn, fpost)

    return (yp, ys.reshape(nb_s, seq_s, D_MODEL), jnp.swapaxes(sp, -1, -2)[None],
            jnp.swapaxes(ss.reshape(1, nb_s, H_A, DV_A, DK_A), -1, -2), cp[None], cs[None],
            vn.reshape(1, nb_s, seq_s, D_B))
```

```python
import math

import jax
import jax.numpy as jnp
import numpy as np
from jax import lax
from jax.experimental import pallas as pl
from jax.experimental.pallas import tpu as pltpu

F32 = jnp.float32
BF16 = jnp.bfloat16

D_MODEL = 1024
H_A, DK_A, DV_A = 8, 128, 64
D_QA, D_VA = H_A * DK_A, H_A * DV_A
H_B, CH_B = 4, 128
D_B = H_B * CH_B
GMLP_CHUNK = 128
D_FF = 2816
CONV_W = 3
EPS = 1e-6

OFF_Q, OFF_F, OFF_I, OFF_OG, OFF_U, OFF_V, OFF_GA, OFF_GB = 0, 1024, 2048, 2560, 3072, 3584, 4096, 5120
D_IN = 6144

HGRN_C = 64
PAIR_K = 2 * DK_A
PAIR_V = 2 * DV_A
N_PAIR = H_A // 2
T_MIX = 512
T_FFN = 1024
FFN_CB = 256
SAMPLE_FFN_CB = 1408
SAMPLE_BB = 32
VMEM_LIMIT = 56 * 1024 * 1024


def _dot(a, b):
    return jnp.dot(a.astype(BF16), b.astype(BF16), preferred_element_type=F32)


def _dot_nt(a, b):
    return lax.dot_general(a.astype(BF16), b.astype(BF16), (((1,), (1,)), ((), ())),
                           preferred_element_type=F32)


def _dot_tn(a, b):
    return lax.dot_general(a.astype(BF16), b.astype(BF16), (((0,), (0,)), ((), ())),
                           preferred_element_type=F32)


def _dot_split(m, x):
    hi = x.astype(BF16)
    lo = (x - hi.astype(F32)).astype(BF16)
    m = m.astype(BF16)
    return (jnp.dot(m, hi, preferred_element_type=F32) + jnp.dot(m, lo, preferred_element_type=F32))


def _rms(x, g):
    return x * lax.rsqrt(jnp.mean(x * x, axis=-1, keepdims=True) + EPS) * g


_LOG2E = math.log2(math.e)
_GELU_K1 = -2.0 * math.sqrt(2.0 / math.pi) * _LOG2E
_GELU_K3 = _GELU_K1 * 0.044715


def _bf16_split(v):
    hi = float(np.asarray(v, np.float32).astype(jnp.bfloat16))
    return hi, float(np.asarray(v - hi, np.float32).astype(jnp.bfloat16))


def _gelu_gate(x):
    s = x * x
    if x.dtype == F32:
        poly = _GELU_K1 + _GELU_K3 * s
    else:
        (k1h, k1l), (k3h, k3l) = _bf16_split(_GELU_K1), _bf16_split(_GELU_K3)
        poly = (k1h + k3h * s) + (k1l + k3l * s)
    return 1.0 / (1.0 + jnp.exp2(x * poly))


def _gelu(x):
    return x * _gelu_gate(x)


def _gelu_mul(x, v):
    return (x * v) * _gelu_gate(x)


def _layer_norm(x, g, b):
    xc = x - jnp.mean(x, axis=-1, keepdims=True)
    return xc * lax.rsqrt(jnp.mean(xc * xc, axis=-1, keepdims=True) + EPS) * g + b


def _lower_bound(lbp):
    e = jnp.exp(lbp - jnp.max(lbp, axis=0, keepdims=True))
    return e[0:1] / jnp.sum(e, axis=0, keepdims=True)


def _sigmoid(x):
    return 1.0 / (1.0 + jnp.exp2(x * -_LOG2E))


def _exp(x, sign=1.0):
    return jnp.exp2(x * (sign * _LOG2E))


def _hgrn_features(q, f_logit, lb):
    f = lb + (1.0 - lb) * _sigmoid(f_logit)
    return q * _sigmoid(q), jnp.log(f), 1.0 - f


def _iota(shape, dim):
    return lax.broadcasted_iota(jnp.int32, shape, dim)


def _div(x, n):
    assert n & (n - 1) == 0
    return lax.shift_right_logical(x, n.bit_length() - 1)


def _mod(x, n):
    assert n & (n - 1) == 0
    return x & (n - 1)


def _ones_where(cond):
    return jnp.where(cond, 1.0, 0.0).astype(BF16)


def _head_mean_square(o):
    same_head = _ones_where(_div(_iota((D_VA, D_VA), 0), DV_A) == _div(_iota((D_VA, D_VA), 1), DV_A))
    return _dot(o * o, same_head) * (1.0 / DV_A)


def _mix_out(x, o, og, ga_sig, hb, hng, wpa_ref, wo_ref, gpost):
    oa = o * lax.rsqrt(_head_mean_square(o) + EPS) * hng * (og * _sigmoid(og))
    h = ga_sig * _dot(oa, wpa_ref[...]) + hb
    return x + _rms(_dot(h, wo_ref[...]), gpost)


def _prompt_mixer_kernel(x_ref, lbp_ref, gpre_ref, win_ref, hng_ref, lng_ref, lnb_ref, ws_ref, bsb_ref,
                         wpa_ref, wpb_ref, wo_ref, gpost_ref, wup_ref, wdn_ref, y_ref, sp_ref,
                         wup16_ref, wdn16_ref, st_ref, qt_ref, kt_ref, qi_ref, kh_ref, iv_ref, d_ref, sc_ref, upd_ref, stb_ref,
                         o_ref, ob_ref):
    wup16_ref[...] = wup_ref[...].astype(BF16)
    wdn16_ref[...] = wdn_ref[...].astype(BF16)
    j = pl.program_id(1)
    t_tile = x_ref.shape[1]
    c = HGRN_C
    n_chunk = t_tile // c

    @pl.when(j == 0)
    def _():
        st_ref[...] = jnp.zeros_like(st_ref)

    x = x_ref[0]
    xn = _rms(x, gpre_ref[...]).astype(BF16)
    lb = _lower_bound(lbp_ref[...])

    def proj(off, width):
        return jnp.dot(xn, win_ref[:, off:off + width], preferred_element_type=F32)

    q_raw, f_raw = proj(OFF_Q, D_QA), proj(OFF_F, D_QA)
    iv_ref[...] = proj(OFF_I, D_VA).astype(BF16)
    u_raw, v_raw, og_raw = proj(OFF_U, D_B), proj(OFF_V, D_B), proj(OFF_OG, D_VA)
    ga_raw, gb_raw = proj(OFF_GA, D_MODEL), proj(OFF_GB, D_MODEL)
    qf, logf, kk = _hgrn_features(q_raw, f_raw, lb)
    r, s = _iota((c + 8, c), 0), _iota((c + 8, c), 1)
    cum_m = jnp.where(r < c, jnp.where(s <= r, 1.0, 0.0) - jnp.where(s < c // 2, 1.0, 0.0), 1.0)
    for ci in range(n_chunk):
        rows = slice(ci * c, (ci + 1) * c)
        lf = logf[rows]
        cum = _dot_split(cum_m, lf)
        a = cum[:c]
        b_last = cum[c:c + 1]
        b_mid = lf[0:1] - a[0:1]
        qt = qf[rows] * _exp(a)
        kt = kk[rows] * _exp(a, -1.0)
        qt_ref[rows, :] = qt.astype(BF16)
        kt_ref[rows, :] = kt.astype(BF16)
        qi_ref[rows, :] = (qt * jnp.exp(b_mid)).astype(BF16)
        kh_ref[rows, :] = (kt * jnp.exp(b_last - b_mid)).astype(BF16)
        d_ref[ci:ci + 1, :] = jnp.exp(b_last)

    kbd_mask = _div(_iota((2 * c, PAIR_K), 0), c) == _div(_iota((2 * c, PAIR_K), 1), DK_A)
    ibd4_mask = _div(_iota((4 * c, 2 * PAIR_V), 0), c) == _div(_iota((4 * c, 2 * PAIR_V), 1), DV_A)
    sbd_mask = _div(_iota((PAIR_V, PAIR_K), 0), DV_A) == _div(_iota((PAIR_V, PAIR_K), 1), DK_A)
    causal = _mod(_iota((c, 2 * c), 1), c) <= _iota((c, 2 * c), 0)
    zero = jnp.zeros((), BF16)

    for p in range(N_PAIR):
        kcols = slice(p * PAIR_K, (p + 1) * PAIR_K)
        vcols = slice(p * PAIR_V, (p + 1) * PAIR_V)
        for ci in range(n_chunk):
            rows = slice(ci * c, (ci + 1) * c)
            kt = kt_ref[rows, kcols]
            kbd = jnp.where(kbd_mask, jnp.concatenate([kt, kt], axis=0), zero)
            sc = lax.dot_general(qt_ref[rows, kcols], kbd, (((1,), (1,)), ((), ())), preferred_element_type=F32)
            sc_ref[rows, vcols] = jnp.where(causal, sc, 0.0).astype(BF16)
            inc = lax.dot_general(iv_ref[rows, vcols], kh_ref[rows, kcols], (((0,), (0,)), ((), ())),
                                  preferred_element_type=F32)
            upd_ref[p * n_chunk + ci] = jnp.where(sbd_mask, inc, 0.0)

    for p in range(N_PAIR):
        kcols = slice(p * PAIR_K, (p + 1) * PAIR_K)
        st = st_ref[p]
        for ci in range(n_chunk):
            stb_ref[p * n_chunk + ci] = st.astype(BF16)
            st = st * d_ref[ci:ci + 1, kcols] + upd_ref[p * n_chunk + ci]
        st_ref[p] = st

    for q4 in range(N_PAIR // 2):
        vcols4 = slice(2 * q4 * PAIR_V, 2 * (q4 + 1) * PAIR_V)
        for ci in range(n_chunk):
            rows = slice(ci * c, (ci + 1) * c)
            ic4 = iv_ref[rows, vcols4]
            ibd4 = jnp.where(ibd4_mask, jnp.concatenate([ic4] * 4, axis=0), zero)
            inter = [lax.dot_general(qi_ref[rows, p * PAIR_K:(p + 1) * PAIR_K], stb_ref[p * n_chunk + ci],
                                     (((1,), (1,)), ((), ())), preferred_element_type=F32)
                     for p in (2 * q4, 2 * q4 + 1)]
            o_ref[rows, vcols4] = (jnp.dot(sc_ref[rows, vcols4], ibd4, preferred_element_type=F32)
                                   + jnp.concatenate(inter, axis=1))

    gu = _gelu(u_raw)
    vn = _layer_norm(_gelu(v_raw), lng_ref[...], lnb_ref[...]).astype(BF16)
    tril = _iota((GMLP_CHUNK, GMLP_CHUNK), 1) <= _iota((GMLP_CHUNK, GMLP_CHUNK), 0)
    zblk = jnp.zeros((GMLP_CHUNK, CH_B), BF16)
    for g2 in range(H_B // 2):
        g0, g1 = 2 * g2, 2 * g2 + 1
        w2 = jnp.concatenate([jnp.where(tril, ws_ref[g0], zero), jnp.where(tril, ws_ref[g1], zero)], axis=1)
        b2 = jnp.concatenate([bsb_ref[g0], bsb_ref[g1]], axis=1)
        cols2 = slice(g0 * CH_B, (g1 + 1) * CH_B)
        for n in range(t_tile // GMLP_CHUNK):
            rows = slice(n * GMLP_CHUNK, (n + 1) * GMLP_CHUNK)
            v2 = vn[rows, cols2]
            vbd = jnp.concatenate([jnp.concatenate([v2[:, :CH_B], zblk], axis=1),
                                   jnp.concatenate([zblk, v2[:, CH_B:]], axis=1)], axis=0)
            ob_ref[rows, cols2] = gu[rows, cols2] * (jnp.dot(w2, vbd, preferred_element_type=F32) + b2)

    hb = _sigmoid(gb_raw) * _dot(ob_ref[...], wpb_ref[...])
    y_ref[0] = _mix_out(x, o_ref[...], og_raw, _sigmoid(ga_raw), hb,
                        hng_ref[...], wpa_ref, wo_ref, gpost_ref[...])

    @pl.when(j == pl.num_programs(1) - 1)
    def _():
        for h in range(H_A):
            hh = h % 2
            sp_ref[0, h] = st_ref[h // 2, hh * DV_A:(hh + 1) * DV_A, hh * DK_A:(hh + 1) * DK_A]


def _const_spec(shape):
    return pl.BlockSpec(shape, lambda *_: (0,) * len(shape), pipeline_mode=pl.Buffered(1))


def _prompt_mixer(x, lbp, gpre, win, hng, lng, lnb, ws, bsb, wpa, wpb, wo, gpost, wup, wdn):
    nb, seq, _ = x.shape
    n_j = seq // T_MIX
    grid = (nb, n_j)
    n_pc = N_PAIR * (T_MIX // HGRN_C)
    weights = (lbp, gpre, win, hng, lng, lnb, ws, bsb, wpa, wpb, wo, gpost)
    up_rows = wup.shape[0] // (nb * n_j)
    dn_rows = wdn.shape[0] // (nb * n_j // 2)
    assert up_rows * nb * n_j == wup.shape[0] and up_rows % 16 == 0
    assert dn_rows * (nb * n_j // 2) == wdn.shape[0] and dn_rows % 16 == 0
    up_spec = pl.BlockSpec((up_rows, wup.shape[1]), lambda b, j: (b * n_j + j, 0))
    dn_spec = pl.BlockSpec((dn_rows, wdn.shape[1]), lambda b, j: ((b * n_j + j) // 2, 0))
    return pl.pallas_call(
        _prompt_mixer_kernel,
        grid=grid,
        in_specs=[pl.BlockSpec((1, T_MIX, D_MODEL), lambda b, j: (b, j, 0))]
        + [_const_spec(w.shape) for w in weights] + [up_spec, dn_spec],
        out_specs=(pl.BlockSpec((1, T_MIX, D_MODEL), lambda b, j: (b, j, 0)),
                   pl.BlockSpec((1, H_A, DV_A, DK_A), lambda b, j: (b, 0, 0, 0)),
                   up_spec, dn_spec),
        out_shape=(jax.ShapeDtypeStruct(x.shape, F32),
                   jax.ShapeDtypeStruct((nb, H_A, DV_A, DK_A), F32),
                   jax.ShapeDtypeStruct(wup.shape, BF16),
                   jax.ShapeDtypeStruct(wdn.shape, BF16)),
        scratch_shapes=[pltpu.VMEM((N_PAIR, PAIR_V, PAIR_K), F32),
                        pltpu.VMEM((T_MIX, D_QA), BF16),
                        pltpu.VMEM((T_MIX, D_QA), BF16),
                        pltpu.VMEM((T_MIX, D_QA), BF16),
                        pltpu.VMEM((T_MIX, D_QA), BF16),
                        pltpu.VMEM((T_MIX, D_VA), BF16),
                        pltpu.VMEM((T_MIX // HGRN_C, D_QA), F32),
                        pltpu.VMEM((T_MIX, D_VA), BF16),
                        pltpu.VMEM((n_pc, PAIR_V, PAIR_K), F32),
                        pltpu.VMEM((n_pc, PAIR_V, PAIR_K), BF16),
                        pltpu.VMEM((T_MIX, D_VA), F32),
                        pltpu.VMEM((T_MIX, D_B), F32)],
        compiler_params=pltpu.CompilerParams(dimension_semantics=("arbitrary", "arbitrary"),
                                             vmem_limit_bytes=VMEM_LIMIT),
        name="prompt_mixer",
    )(x, *weights, wup, wdn)


def _prompt_ffn_kernel(x_ref, gpre_ref, wup_ref, cw_ref, cb_ref, wdn_ref, gpost_ref, y_ref, cp_ref,
                       tail_ref, a_ref):
    j = pl.program_id(1)
    t_tile = x_ref.shape[1]

    @pl.when(j == 0)
    def _():
        tail_ref[...] = jnp.zeros_like(tail_ref)

    x = x_ref[0]
    xn = _rms(x, gpre_ref[...]).astype(BF16)
    row = _iota((t_tile, FFN_CB), 0)

    def up_proj(c0):
        return jnp.dot(xn, wup_ref[:, c0:c0 + FFN_CB], preferred_element_type=F32)

    def conv(c0, up):
        cols = slice(c0, c0 + FFN_CB)
        p0 = tail_ref[6:7, cols]
        p1 = tail_ref[7:8, cols]
        tail_ref[:, cols] = up[t_tile - 8:]
        m1 = jnp.where(row == 0, p1, pltpu.roll(up, 1, 0))
        m2 = jnp.where(row == 0, p0, jnp.where(row == 1, p1, pltpu.roll(up, 2, 0)))
        w0, w1, w2 = (cw_ref[k:k + 1, cols].astype(BF16) for k in range(CONV_W))
        return (cb_ref[:, cols].astype(BF16) + w0 * m2.astype(BF16) + w1 * m1.astype(BF16)
                + w2 * up.astype(BF16))

    nblk = D_FF // FFN_CB
    ups = (up_proj(0), up_proj(D_FF))
    for blk in range(nblk):
        c0 = blk * FFN_CB
        nxt = (up_proj(c0 + FFN_CB), up_proj(D_FF + c0 + FFN_CB)) if blk + 1 < nblk else None
        a_ref[:, c0:c0 + FFN_CB] = _gelu_mul(conv(c0, ups[0]), conv(D_FF + c0, ups[1])).astype(BF16)
        ups = nxt

    y_ref[0] = x + _rms(jnp.dot(a_ref[...], wdn_ref[...], preferred_element_type=F32), gpost_ref[...])

    @pl.when(j == pl.num_programs(1) - 1)
    def _():
        cp_ref[0] = tail_ref[6:8, :]


def _prompt_ffn(x, gpre, wup, cw, cb, wdn, gpost):
    nb, seq, _ = x.shape
    weights = (gpre, wup, cw, cb, wdn, gpost)
    return pl.pallas_call(
        _prompt_ffn_kernel,
        grid=(nb, seq // T_FFN),
        in_specs=[pl.BlockSpec((1, T_FFN, D_MODEL), lambda b, j: (b, j, 0))]
        + [_const_spec(w.shape) for w in weights],
        out_specs=(pl.BlockSpec((1, T_FFN, D_MODEL), lambda b, j: (b, j, 0)),
                   pl.BlockSpec((1, CONV_W - 1, 2 * D_FF), lambda b, j: (b, 0, 0))),
        out_shape=(jax.ShapeDtypeStruct(x.shape, F32),
                   jax.ShapeDtypeStruct((nb, CONV_W - 1, 2 * D_FF), F32)),
        scratch_shapes=[pltpu.VMEM((8, 2 * D_FF), F32),
                        pltpu.VMEM((T_FFN, D_FF), BF16)],
        compiler_params=pltpu.CompilerParams(dimension_semantics=("arbitrary", "arbitrary"),
                                             vmem_limit_bytes=VMEM_LIMIT),
        name="prompt_ffn",
    )(x, *weights)


def _sample_pre_kernel(wsm_ref, bsm_ref, x_ref, lbp_ref, gpre_ref, win_ref, lng_ref, lnb_ref, wpb_ref,
                       qi_ref, kh_ref, d_ref, i_ref, oin_ref, og_ref, gas_ref, hb_ref, vn_ref):
    n_rows = x_ref.shape[0]
    seq = 4
    x = x_ref[...]
    xn = _rms(x, gpre_ref[...]).astype(BF16)
    lb = _lower_bound(lbp_ref[...])

    def proj(off, width):
        return jnp.dot(xn, win_ref[:, off:off + width], preferred_element_type=F32)

    def tok(width):
        return _mod(_iota((n_rows, width), 0), seq)

    def shift(v, k):
        return pltpu.roll(v, k, 0) if k else v

    qf, logf, kk = _hgrn_features(proj(OFF_Q, D_QA), proj(OFF_F, D_QA), lb)
    iv = proj(OFF_I, D_VA)
    tq = tok(D_QA)
    b = logf + jnp.where(tq >= 1, shift(logf, 1), 0.0)
    b = b + jnp.where(tq >= 2, shift(b, 2), 0.0)
    b_last = jnp.where(tq == seq - 1, b, 0.0)
    b_last = b_last + pltpu.roll(b_last, n_rows - 1, 0)
    b_last = b_last + pltpu.roll(b_last, n_rows - 2, 0)
    qi_ref[...] = qf * jnp.exp(b)
    kh_ref[...] = kk * jnp.exp(b_last - b)
    d_ref[...] = jnp.exp(b_last)
    i_ref[...] = iv

    head_sum = _ones_where(_div(_iota((D_QA, D_VA), 0), DK_A) == _div(_iota((D_QA, D_VA), 1), DV_A))
    tv = tok(D_VA)
    o = jnp.zeros((n_rows, D_VA), F32)
    for k in range(seq):
        pair = qf * shift(kk, k) * jnp.exp(b - shift(b, k))
        sc = jnp.dot(jnp.where(tq >= k, pair, 0.0).astype(BF16), head_sum, preferred_element_type=F32)
        o = o + jnp.where(tv >= k, sc * shift(iv, k), 0.0)
    oin_ref[...] = o

    gu = _gelu(proj(OFF_U, D_B))
    vn = _layer_norm(_gelu(proj(OFF_V, D_B)), lng_ref[...], lnb_ref[...])
    vn_ref[...] = vn
    tg = tok(CH_B)
    obs = []
    for g in range(H_B):
        vg = vn[:, g * CH_B:(g + 1) * CH_B]
        s = jnp.zeros((n_rows, CH_B), F32)
        for t in range(seq):
            s = s + jnp.where(tg == t, bsm_ref[g * seq + t], 0.0)
        for k in range(seq):
            coef = jnp.zeros((n_rows, CH_B), F32)
            for t in range(k, seq):
                coef = coef + jnp.where(tg == t, wsm_ref[(g * seq + t) * seq + (t - k)], 0.0)
            s = s + coef * shift(vg, k)
        obs.append(gu[:, g * CH_B:(g + 1) * CH_B] * s)
    ob = jnp.concatenate(obs, axis=1)
    hb_ref[...] = _sigmoid(proj(OFF_GB, D_MODEL)) * _dot(ob, wpb_ref[...])
    gas_ref[...] = _sigmoid(proj(OFF_GA, D_MODEL))
    og_ref[...] = proj(OFF_OG, D_VA)


def _sample_pre(wsm, bsm, x, lbp, gpre, win, lng, lnb, wpb):
    n_rows = x.shape[0]
    shapes = [(n_rows, D_QA), (n_rows, D_QA), (n_rows, D_QA), (n_rows, D_VA), (n_rows, D_VA),
              (n_rows, D_VA), (n_rows, D_MODEL), (n_rows, D_MODEL), (n_rows, D_B)]
    smem = pl.BlockSpec(memory_space=pltpu.SMEM)
    return pl.pallas_call(
        _sample_pre_kernel,
        in_specs=[smem, smem] + [pl.BlockSpec(memory_space=pltpu.VMEM)] * 7,
        out_shape=tuple(jax.ShapeDtypeStruct(s, F32) for s in shapes),
        compiler_params=pltpu.CompilerParams(vmem_limit_bytes=VMEM_LIMIT),
        name="sample_pre",
    )(wsm, bsm, x, lbp, gpre, win, lng, lnb, wpb)


def _sample_state_kernel(qi_ref, kh_ref, d_ref, i_ref, s_ref, o_ref, sn_ref):
    seq = 4
    nseq = 4
    blk = nseq * seq
    rows_seq = _div(_iota((blk, DV_A), 0), seq)

    @pl.loop(0, qi_ref.shape[0] // blk)
    def _(m):
        row0 = pl.multiple_of(m * blk, blk)
        rows = pl.ds(row0, blk)
        outs = []
        for h in range(H_A):
            kcols = slice(h * DK_A, (h + 1) * DK_A)
            vcols = slice(h * DV_A, (h + 1) * DV_A)
            idx = [(m * nseq + bb) * H_A + h for bb in range(nseq)]
            st = [s_ref[i] for i in idx]
            q16 = qi_ref[rows, kcols].astype(BF16)
            r = lax.dot_general(q16, jnp.concatenate(st, axis=0).astype(BF16), (((1,), (1,)), ((), ())),
                                preferred_element_type=F32)
            oh = r[:, :DV_A]
            for bb in range(1, nseq):
                oh = jnp.where(rows_seq == bb, r[:, bb * DV_A:(bb + 1) * DV_A], oh)
            outs.append(oh)
            i16 = i_ref[rows, vcols]
            i_sel = jnp.concatenate([jnp.where(rows_seq == bb, i16, 0.0) for bb in range(nseq)], axis=1)
            inc = _dot_tn(i_sel, kh_ref[rows, kcols])
            d16 = d_ref[rows, kcols]
            for bb in range(nseq):
                sn_ref[idx[bb]] = st[bb] * d16[bb * seq:bb * seq + 1] + inc[bb * DV_A:(bb + 1) * DV_A]
        o_ref[rows, :] = jnp.concatenate(outs, axis=1)


def _sample_state(qi, kh, d, iv, state):
    n_rows = qi.shape[0]
    rb = SAMPLE_BB * 4
    sb = SAMPLE_BB * H_A
    return pl.pallas_call(
        _sample_state_kernel,
        grid=(n_rows // rb,),
        in_specs=[pl.BlockSpec((rb, D_QA), lambda i: (i, 0)),
                  pl.BlockSpec((rb, D_QA), lambda i: (i, 0)),
                  pl.BlockSpec((rb, D_QA), lambda i: (i, 0)),
                  pl.BlockSpec((rb, D_VA), lambda i: (i, 0)),
                  pl.BlockSpec((sb, DV_A, DK_A), lambda i: (i, 0, 0))],
        out_specs=(pl.BlockSpec((rb, D_VA), lambda i: (i, 0)),
                   pl.BlockSpec((sb, DV_A, DK_A), lambda i: (i, 0, 0))),
        out_shape=(jax.ShapeDtypeStruct((n_rows, D_VA), F32),
                   jax.ShapeDtypeStruct(state.shape, F32)),
        compiler_params=pltpu.CompilerParams(dimension_semantics=("arbitrary",),
                                             vmem_limit_bytes=VMEM_LIMIT),
        name="sample_state",
    )(qi, kh, d, iv, state)


def _sample_post_kernel(x_ref, oin_ref, oint_ref, og_ref, gas_ref, hb_ref, hng_ref, wpa_ref, wo_ref,
                        gpost_ref, y_ref):
    y_ref[...] = _mix_out(x_ref[...], oin_ref[...] + oint_ref[...], og_ref[...], gas_ref[...], hb_ref[...],
                          hng_ref[...], wpa_ref, wo_ref, gpost_ref[...])


def _sample_post(x, oin, oint, og, gas, hb, hng, wpa, wo, gpost):
    return pl.pallas_call(
        _sample_post_kernel,
        out_shape=jax.ShapeDtypeStruct(x.shape, F32),
        compiler_params=pltpu.CompilerParams(vmem_limit_bytes=VMEM_LIMIT),
        name="sample_post",
    )(x, oin, oint, og, gas, hb, hng, wpa, wo, gpost)


def _sample_ffn_kernel(x_ref, gpre_ref, wu_ref, cache_ref, cw_ref, cb_ref, wdn_ref, gpost_ref, y_ref, cs_ref,
                       xn_ref, acc_ref, old_ref, up_ref, val_ref):
    step = pl.program_id(0)
    nblk = pl.num_programs(0) // 2
    n_rows = x_ref.shape[0]
    n_seq = cache_ref.shape[0]
    seq = n_rows // n_seq
    keep = CONV_W - 1

    @pl.when(step == 0)
    def _():
        xn_ref[...] = _rms(x_ref[...], gpre_ref[...]).astype(BF16)
        acc_ref[...] = jnp.zeros_like(acc_ref)
        old_ref[...] = jnp.zeros_like(old_ref)

    up = jnp.dot(xn_ref[...], wu_ref[...], preferred_element_type=F32)
    up_ref[...] = up
    for b in range(n_seq):
        old_ref[seq * b:seq * b + keep, :] = cache_ref[b]
        cs_ref[b] = up_ref[seq * (b + 1) - keep:seq * (b + 1), :]
    old = old_ref[...]
    tok = _mod(_iota(up.shape, 0), seq)
    m2 = jnp.where(tok >= 2, pltpu.roll(up, 2, 0), old)
    m1 = jnp.where(tok >= 1, pltpu.roll(up, 1, 0), pltpu.roll(old, n_rows - 1, 0))
    conv = cb_ref[...] + cw_ref[0:1, :] * m2 + cw_ref[1:2, :] * m1 + cw_ref[2:3, :] * up

    @pl.when(step < nblk)
    def _():
        val_ref[step] = conv

    @pl.when(step >= nblk)
    def _():
        a = _gelu_mul(conv, val_ref[step - nblk])
        acc_ref[...] += _dot(a, wdn_ref[...])

    @pl.when(step == pl.num_programs(0) - 1)
    def _():
        y_ref[...] = x_ref[...] + _rms(acc_ref[...], gpost_ref[...])


def _sample_ffn(x, gpre, wup, cache, cw, cb, wdn, gpost):
    n_rows = x.shape[0]
    n_seq = cache.shape[0]
    nblk = D_FF // SAMPLE_FFN_CB
    col = lambda i: (0, lax.rem(i + nblk, 2 * nblk))
    col3 = lambda i: (0, 0, lax.rem(i + nblk, 2 * nblk))
    full = lambda shape: pl.BlockSpec(shape, lambda i: (0,) * len(shape))
    return pl.pallas_call(
        _sample_ffn_kernel,
        grid=(2 * nblk,),
        in_specs=[full((n_rows, D_MODEL)), full((1, D_MODEL)),
                  pl.BlockSpec((D_MODEL, SAMPLE_FFN_CB), col),
                  pl.BlockSpec((n_seq, CONV_W - 1, SAMPLE_FFN_CB), col3),
                  pl.BlockSpec((CONV_W, SAMPLE_FFN_CB), col),
                  pl.BlockSpec((1, SAMPLE_FFN_CB), col),
                  pl.BlockSpec((SAMPLE_FFN_CB, D_MODEL), lambda i: (jnp.maximum(i - nblk, 0), 0)),
                  full((1, D_MODEL))],
        out_specs=(full((n_rows, D_MODEL)),
                   pl.BlockSpec((n_seq, CONV_W - 1, SAMPLE_FFN_CB), col3)),
        out_shape=(jax.ShapeDtypeStruct((n_rows, D_MODEL), F32),
                   jax.ShapeDtypeStruct(cache.shape, F32)),
        scratch_shapes=[pltpu.VMEM((n_rows, D_MODEL), BF16), pltpu.VMEM((n_rows, D_MODEL), F32),
                        pltpu.VMEM((n_rows, SAMPLE_FFN_CB), F32), pltpu.VMEM((n_rows, SAMPLE_FFN_CB), F32),
                        pltpu.VMEM((nblk, n_rows, SAMPLE_FFN_CB), F32)],
        compiler_params=pltpu.CompilerParams(dimension_semantics=("arbitrary",),
                                             vmem_limit_bytes=VMEM_LIMIT),
        name="sample_ffn",
    )(x, gpre, wup, cache, cw, cb, wdn, gpost)


def kernel(x_prompt, x_sample, state_hgrn, cache_ffn_conv, lb_param, mix_pre_g, w_in, hgrn_norm_g, gmlp_ln_g, gmlp_ln_b, w_s, b_s, w_pa, w_pb, w_o, mix_post_g, ffn_pre_g, w_up, conv_w, conv_b, w_down, ffn_post_g):
    nb_s, seq_s, _ = x_sample.shape
    row = lambda v: v.reshape(1, -1)
    win, wpa, wpb, wo = (w[0].astype(BF16) for w in (w_in, w_pa, w_pb, w_o))
    ws = w_s[0].astype(BF16)
    gpre, gpost, fpre, fpost = row(mix_pre_g[0]), row(mix_post_g[0]), row(ffn_pre_g[0]), row(ffn_post_g[0])
    lng, lnb, cb = row(gmlp_ln_g[0]), row(gmlp_ln_b[0]), row(conv_b[0])
    hng = row(jnp.tile(hgrn_norm_g[0], H_A))
    bsb = jnp.broadcast_to(b_s[0][:, :, None], (H_B, GMLP_CHUNK, CH_B))

    x1, sp, wup, wdn = _prompt_mixer(x_prompt, lb_param, gpre, win, hng, lng, lnb, ws, bsb, wpa, wpb, wo, gpost,
                                     w_up[0], w_down[0])
    yp, cp = _prompt_ffn(x1, fpre, wup, conv_w[0], cb, wdn, fpost)

    xs = x_sample.reshape(nb_s * seq_s, D_MODEL)
    wsm = w_s[0][:, :seq_s, :seq_s].reshape(-1)
    bsm = b_s[0][:, :seq_s].reshape(-1)
    qi, kh, d, iv, oin, og, gas, hb, vn = _sample_pre(wsm, bsm, xs, lb_param, gpre, win, lng, lnb, wpb)
    st_in = jnp.swapaxes(state_hgrn[0], -1, -2).reshape(nb_s * H_A, DV_A, DK_A)
    oint, ss = _sample_state(qi, kh, d, iv, st_in)
    xs1 = _sample_post(xs, oin, oint, og, gas, hb, hng, wpa, wo, gpost)
    ys, cs = _sample_ffn(xs1, fpre, wup, cache_ffn_conv[0], conv_w[0], cb, wdn, fpost)

    return (yp, ys.reshape(nb_s, seq_s, D_MODEL), jnp.swapaxes(sp, -1, -2)[None],
            jnp.swapaxes(ss.reshape(1, nb_s, H_A, DV_A, DK_A), -1, -2), cp[None], cs[None],
            vn.reshape(1, nb_s, seq_s, D_B))
```

```python
import math

import jax
import jax.numpy as jnp
import numpy as np
from jax import lax
from jax.experimental import pallas as pl
from jax.experimental.pallas import tpu as pltpu

F32 = jnp.float32
BF16 = jnp.bfloat16

D_MODEL = 1024
H_A, DK_A, DV_A = 8, 128, 64
D_QA, D_VA = H_A * DK_A, H_A * DV_A
H_B, CH_B = 4, 128
D_B = H_B * CH_B
GMLP_CHUNK = 128
D_FF = 2816
CONV_W = 3
EPS = 1e-6

OFF_Q, OFF_F, OFF_I, OFF_OG, OFF_U, OFF_V, OFF_GA, OFF_GB = 0, 1024, 2048, 2560, 3072, 3584, 4096, 5120
D_IN = 6144

HGRN_C = 64
PAIR_K = 2 * DK_A
PAIR_V = 2 * DV_A
N_PAIR = H_A // 2
T_MIX = 512
T_FFN = 1024
FFN_CB = 256
SAMPLE_FFN_CB = 1408
SAMPLE_BB = 32
VMEM_LIMIT = 56 * 1024 * 1024


def _dot(a, b):
    return jnp.dot(a.astype(BF16), b.astype(BF16), preferred_element_type=F32)


def _dot_nt(a, b):
    return lax.dot_general(a.astype(BF16), b.astype(BF16), (((1,), (1,)), ((), ())),
                           preferred_element_type=F32)


def _dot_tn(a, b):
    return lax.dot_general(a.astype(BF16), b.astype(BF16), (((0,), (0,)), ((), ())),
                           preferred_element_type=F32)


def _dot_split(m, x):
    hi = x.astype(BF16)
    lo = (x - hi.astype(F32)).astype(BF16)
    m = m.astype(BF16)
    return (jnp.dot(m, hi, preferred_element_type=F32) + jnp.dot(m, lo, preferred_element_type=F32))


def _rms(x, g):
    return x * lax.rsqrt(jnp.mean(x * x, axis=-1, keepdims=True) + EPS) * g


_LOG2E = math.log2(math.e)
_GELU_K1 = -2.0 * math.sqrt(2.0 / math.pi) * _LOG2E
_GELU_K3 = _GELU_K1 * 0.044715


def _bf16_split(v):
    hi = float(np.asarray(v, np.float32).astype(jnp.bfloat16))
    return hi, float(np.asarray(v - hi, np.float32).astype(jnp.bfloat16))


def _gelu_gate(x):
    s = x * x
    if x.dtype == F32:
        poly = _GELU_K1 + _GELU_K3 * s
    else:
        (k1h, k1l), (k3h, k3l) = _bf16_split(_GELU_K1), _bf16_split(_GELU_K3)
        poly = (k1h + k3h * s) + (k1l + k3l * s)
    return 1.0 / (1.0 + jnp.exp2(x * poly))


def _gelu(x):
    return x * _gelu_gate(x)


def _gelu_mul(x, v):
    return (x * v) * _gelu_gate(x)


def _layer_norm(x, g, b):
    xc = x - jnp.mean(x, axis=-1, keepdims=True)
    return xc * lax.rsqrt(jnp.mean(xc * xc, axis=-1, keepdims=True) + EPS) * g + b


def _lower_bound(lbp):
    e = jnp.exp(lbp - jnp.max(lbp, axis=0, keepdims=True))
    return e[0:1] / jnp.sum(e, axis=0, keepdims=True)


def _sigmoid(x):
    return 1.0 / (1.0 + jnp.exp2(x * -_LOG2E))


def _exp(x, sign=1.0):
    return jnp.exp2(x * (sign * _LOG2E))


def _hgrn_features(q, f_logit, lb):
    f = lb + (1.0 - lb) * _sigmoid(f_logit)
    return q * _sigmoid(q), jnp.log(f), 1.0 - f


def _iota(shape, dim):
    return lax.broadcasted_iota(jnp.int32, shape, dim)


def _div(x, n):
    assert n & (n - 1) == 0
    return lax.shift_right_logical(x, n.bit_length() - 1)


def _mod(x, n):
    assert n & (n - 1) == 0
    return x & (n - 1)


def _ones_where(cond):
    return jnp.where(cond, 1.0, 0.0).astype(BF16)


def _head_mean_square(o):
    same_head = _ones_where(_div(_iota((D_VA, D_VA), 0), DV_A) == _div(_iota((D_VA, D_VA), 1), DV_A))
    return _dot(o * o, same_head) * (1.0 / DV_A)


def _mix_out(x, o, og, ga_sig, hb, hng, wpa_ref, wo_ref, gpost):
    oa = o * lax.rsqrt(_head_mean_square(o) + EPS) * hng * (og * _sigmoid(og))
    h = ga_sig * _dot(oa, wpa_ref[...]) + hb
    return x + _rms(_dot(h, wo_ref[...]), gpost)


def _prompt_mixer_kernel(x_ref, lbp_ref, gpre_ref, win_ref, hng_ref, lng_ref, lnb_ref, ws_ref, bsb_ref,
                         wpa_ref, wpb_ref, wo_ref, gpost_ref, wup_ref, wdn_ref, y_ref, sp_ref,
                         wup16_ref, wdn16_ref, st_ref, qt_ref, kt_ref, qi_ref, kh_ref, iv_ref, d_ref, sc_ref, upd_ref, stb_ref,
                         o_ref, ob_ref):
    wup16_ref[...] = wup_ref[...].astype(BF16)
    wdn16_ref[...] = wdn_ref[...].astype(BF16)
    j = pl.program_id(1)
    t_tile = x_ref.shape[1]
    c = HGRN_C
    n_chunk = t_tile // c

    @pl.when(j == 0)
    def _():
        st_ref[...] = jnp.zeros_like(st_ref)

    x = x_ref[0]
    xn = _rms(x, gpre_ref[...]).astype(BF16)
    lb = _lower_bound(lbp_ref[...])

    def proj(off, width):
        return jnp.dot(xn, win_ref[:, off:off + width], preferred_element_type=F32)

    q_raw, f_raw = proj(OFF_Q, D_QA), proj(OFF_F, D_QA)
    iv_ref[...] = proj(OFF_I, D_VA).astype(BF16)
    qf, logf, kk = _hgrn_features(q_raw, f_raw, lb)
    r, s = _iota((c + 8, c), 0), _iota((c + 8, c), 1)
    cum_m = jnp.where(r < c, jnp.where(s <= r, 1.0, 0.0) - jnp.where(s < c // 2, 1.0, 0.0), 1.0)
    for ci in range(n_chunk):
        rows = slice(ci * c, (ci + 1) * c)
        lf = logf[rows]
        cum = _dot_split(cum_m, lf)
        a = cum[:c]
        b_last = cum[c:c + 1]
        b_mid = lf[0:1] - a[0:1]
        qt = qf[rows] * _exp(a)
        kt = kk[rows] * _exp(a, -1.0)
        qt_ref[rows, :] = qt.astype(BF16)
        kt_ref[rows, :] = kt.astype(BF16)
        qi_ref[rows, :] = (qt * jnp.exp(b_mid)).astype(BF16)
        kh_ref[rows, :] = (kt * jnp.exp(b_last - b_mid)).astype(BF16)
        d_ref[ci:ci + 1, :] = jnp.exp(b_last)

    u_raw, v_raw, og_raw = proj(OFF_U, D_B), proj(OFF_V, D_B), proj(OFF_OG, D_VA)
    ga_raw, gb_raw = proj(OFF_GA, D_MODEL), proj(OFF_GB, D_MODEL)

    kbd_mask = _div(_iota((2 * c, PAIR_K), 0), c) == _div(_iota((2 * c, PAIR_K), 1), DK_A)
    ibd4_mask = _div(_iota((4 * c, 2 * PAIR_V), 0), c) == _div(_iota((4 * c, 2 * PAIR_V), 1), DV_A)
    sbd_mask = _div(_iota((PAIR_V, PAIR_K), 0), DV_A) == _div(_iota((PAIR_V, PAIR_K), 1), DK_A)
    causal = _mod(_iota((c, 2 * c), 1), c) <= _iota((c, 2 * c), 0)
    zero = jnp.zeros((), BF16)

    for p in range(N_PAIR):
        kcols = slice(p * PAIR_K, (p + 1) * PAIR_K)
        vcols = slice(p * PAIR_V, (p + 1) * PAIR_V)
        for ci in range(n_chunk):
            rows = slice(ci * c, (ci + 1) * c)
            kt = kt_ref[rows, kcols]
            kbd = jnp.where(kbd_mask, jnp.concatenate([kt, kt], axis=0), zero)
            sc = lax.dot_general(qt_ref[rows, kcols], kbd, (((1,), (1,)), ((), ())), preferred_element_type=F32)
            sc_ref[rows, vcols] = jnp.where(causal, sc, 0.0).astype(BF16)
            inc = lax.dot_general(iv_ref[rows, vcols], kh_ref[rows, kcols], (((0,), (0,)), ((), ())),
                                  preferred_element_type=F32)
            upd_ref[p * n_chunk + ci] = jnp.where(sbd_mask, inc, 0.0)

    for p in range(N_PAIR):
        kcols = slice(p * PAIR_K, (p + 1) * PAIR_K)
        st = st_ref[p]
        for ci in range(n_chunk):
            stb_ref[p * n_chunk + ci] = st.astype(BF16)
            st = st * d_ref[ci:ci + 1, kcols] + upd_ref[p * n_chunk + ci]
        st_ref[p] = st

    for q4 in range(N_PAIR // 2):
        vcols4 = slice(2 * q4 * PAIR_V, 2 * (q4 + 1) * PAIR_V)
        for ci in range(n_chunk):
            rows = slice(ci * c, (ci + 1) * c)
            ic4 = iv_ref[rows, vcols4]
            ibd4 = jnp.where(ibd4_mask, jnp.concatenate([ic4] * 4, axis=0), zero)
            inter = [lax.dot_general(qi_ref[rows, p * PAIR_K:(p + 1) * PAIR_K], stb_ref[p * n_chunk + ci],
                                     (((1,), (1,)), ((), ())), preferred_element_type=F32)
                     for p in (2 * q4, 2 * q4 + 1)]
            o_ref[rows, vcols4] = (jnp.dot(sc_ref[rows, vcols4], ibd4, preferred_element_type=F32)
                                   + jnp.concatenate(inter, axis=1))

    gu = _gelu(u_raw)
    vn = _layer_norm(_gelu(v_raw), lng_ref[...], lnb_ref[...]).astype(BF16)
    tril = _iota((GMLP_CHUNK, GMLP_CHUNK), 1) <= _iota((GMLP_CHUNK, GMLP_CHUNK), 0)
    zblk = jnp.zeros((GMLP_CHUNK, CH_B), BF16)
    for g2 in range(H_B // 2):
        g0, g1 = 2 * g2, 2 * g2 + 1
        w2 = jnp.concatenate([jnp.where(tril, ws_ref[g0], zero), jnp.where(tril, ws_ref[g1], zero)], axis=1)
        b2 = jnp.concatenate([bsb_ref[g0], bsb_ref[g1]], axis=1)
        cols2 = slice(g0 * CH_B, (g1 + 1) * CH_B)
        for n in range(t_tile // GMLP_CHUNK):
            rows = slice(n * GMLP_CHUNK, (n + 1) * GMLP_CHUNK)
            v2 = vn[rows, cols2]
            vbd = jnp.concatenate([jnp.concatenate([v2[:, :CH_B], zblk], axis=1),
                                   jnp.concatenate([zblk, v2[:, CH_B:]], axis=1)], axis=0)
            ob_ref[rows, cols2] = gu[rows, cols2] * (jnp.dot(w2, vbd, preferred_element_type=F32) + b2)

    hb = _sigmoid(gb_raw) * _dot(ob_ref[...], wpb_ref[...])
    y_ref[0] = _mix_out(x, o_ref[...], og_raw, _sigmoid(ga_raw), hb,
                        hng_ref[...], wpa_ref, wo_ref, gpost_ref[...])

    @pl.when(j == pl.num_programs(1) - 1)
    def _():
        for h in range(H_A):
            hh = h % 2
            sp_ref[0, h] = st_ref[h // 2, hh * DV_A:(hh + 1) * DV_A, hh * DK_A:(hh + 1) * DK_A]


def _const_spec(shape):
    return pl.BlockSpec(shape, lambda *_: (0,) * len(shape), pipeline_mode=pl.Buffered(1))


def _prompt_mixer(x, lbp, gpre, win, hng, lng, lnb, ws, bsb, wpa, wpb, wo, gpost, wup, wdn):
    nb, seq, _ = x.shape
    n_j = seq // T_MIX
    grid = (nb, n_j)
    n_pc = N_PAIR * (T_MIX // HGRN_C)
    weights = (lbp, gpre, win, hng, lng, lnb, ws, bsb, wpa, wpb, wo, gpost)
    up_rows = wup.shape[0] // (nb * n_j)
    dn_rows = wdn.shape[0] // (nb * n_j // 2)
    assert up_rows * nb * n_j == wup.shape[0] and up_rows % 16 == 0
    assert dn_rows * (nb * n_j // 2) == wdn.shape[0] and dn_rows % 16 == 0
    up_spec = pl.BlockSpec((up_rows, wup.shape[1]), lambda b, j: (b * n_j + j, 0))
    dn_spec = pl.BlockSpec((dn_rows, wdn.shape[1]), lambda b, j: ((b * n_j + j) // 2, 0))
    return pl.pallas_call(
        _prompt_mixer_kernel,
        grid=grid,
        in_specs=[pl.BlockSpec((1, T_MIX, D_MODEL), lambda b, j: (b, j, 0))]
        + [_const_spec(w.shape) for w in weights] + [up_spec, dn_spec],
        out_specs=(pl.BlockSpec((1, T_MIX, D_MODEL), lambda b, j: (b, j, 0)),
                   pl.BlockSpec((1, H_A, DV_A, DK_A), lambda b, j: (b, 0, 0, 0)),
                   up_spec, dn_spec),
        out_shape=(jax.ShapeDtypeStruct(x.shape, F32),
                   jax.ShapeDtypeStruct((nb, H_A, DV_A, DK_A), F32),
                   jax.ShapeDtypeStruct(wup.shape, BF16),
                   jax.ShapeDtypeStruct(wdn.shape, BF16)),
        scratch_shapes=[pltpu.VMEM((N_PAIR, PAIR_V, PAIR_K), F32),
                        pltpu.VMEM((T_MIX, D_QA), BF16),
                        pltpu.VMEM((T_MIX, D_QA), BF16),
                        pltpu.VMEM((T_MIX, D_QA), BF16),
                        pltpu.VMEM((T_MIX, D_QA), BF16),
                        pltpu.VMEM((T_MIX, D_VA), BF16),
                        pltpu.VMEM((T_MIX // HGRN_C, D_QA), F32),
                        pltpu.VMEM((T_MIX, D_VA), BF16),
                        pltpu.VMEM((n_pc, PAIR_V, PAIR_K), F32),
                        pltpu.VMEM((n_pc, PAIR_V, PAIR_K), BF16),
                        pltpu.VMEM((T_MIX, D_VA), F32),
                        pltpu.VMEM((T_MIX, D_B), F32)],
        compiler_params=pltpu.CompilerParams(dimension_semantics=("arbitrary", "arbitrary"),
                                             vmem_limit_bytes=VMEM_LIMIT),
        name="prompt_mixer",
    )(x, *weights, wup, wdn)


def _prompt_ffn_kernel(x_ref, gpre_ref, wup_ref, cw_ref, cb_ref, wdn_ref, gpost_ref, y_ref, cp_ref,
                       tail_ref, a_ref):
    j = pl.program_id(1)
    t_tile = x_ref.shape[1]

    @pl.when(j == 0)
    def _():
        tail_ref[...] = jnp.zeros_like(tail_ref)

    x = x_ref[0]
    xn = _rms(x, gpre_ref[...]).astype(BF16)
    row = _iota((t_tile, FFN_CB), 0)

    def up_proj(c0):
        return jnp.dot(xn, wup_ref[:, c0:c0 + FFN_CB], preferred_element_type=F32)

    def conv(c0, up):
        cols = slice(c0, c0 + FFN_CB)
        p0 = tail_ref[6:7, cols]
        p1 = tail_ref[7:8, cols]
        tail_ref[:, cols] = up[t_tile - 8:]
        m1 = jnp.where(row == 0, p1, pltpu.roll(up, 1, 0))
        m2 = jnp.where(row == 0, p0, jnp.where(row == 1, p1, pltpu.roll(up, 2, 0)))
        w0, w1, w2 = (cw_ref[k:k + 1, cols].astype(BF16) for k in range(CONV_W))
        return (cb_ref[:, cols].astype(BF16) + w0 * m2.astype(BF16) + w1 * m1.astype(BF16)
                + w2 * up.astype(BF16))

    nblk = D_FF // FFN_CB
    ups = (up_proj(0), up_proj(D_FF))
    for blk in range(nblk):
        c0 = blk * FFN_CB
        nxt = (up_proj(c0 + FFN_CB), up_proj(D_FF + c0 + FFN_CB)) if blk + 1 < nblk else None
        a_ref[:, c0:c0 + FFN_CB] = _gelu_mul(conv(c0, ups[0]), conv(D_FF + c0, ups[1])).astype(BF16)
        ups = nxt

    y_ref[0] = x + _rms(jnp.dot(a_ref[...], wdn_ref[...], preferred_element_type=F32), gpost_ref[...])

    @pl.when(j == pl.num_programs(1) - 1)
    def _():
        cp_ref[0] = tail_ref[6:8, :]


def _prompt_ffn(x, gpre, wup, cw, cb, wdn, gpost):
    nb, seq, _ = x.shape
    weights = (gpre, wup, cw, cb, wdn, gpost)
    return pl.pallas_call(
        _prompt_ffn_kernel,
        grid=(nb, seq // T_FFN),
        in_specs=[pl.BlockSpec((1, T_FFN, D_MODEL), lambda b, j: (b, j, 0))]
        + [_const_spec(w.shape) for w in weights],
        out_specs=(pl.BlockSpec((1, T_FFN, D_MODEL), lambda b, j: (b, j, 0)),
                   pl.BlockSpec((1, CONV_W - 1, 2 * D_FF), lambda b, j: (b, 0, 0))),
        out_shape=(jax.ShapeDtypeStruct(x.shape, F32),
                   jax.ShapeDtypeStruct((nb, CONV_W - 1, 2 * D_FF), F32)),
        scratch_shapes=[pltpu.VMEM((8, 2 * D_FF), F32),
                        pltpu.VMEM((T_FFN, D_FF), BF16)],
        compiler_params=pltpu.CompilerParams(dimension_semantics=("arbitrary", "arbitrary"),
                                             vmem_limit_bytes=VMEM_LIMIT),
        name="prompt_ffn",
    )(x, *weights)


def _sample_pre_kernel(wsm_ref, bsm_ref, x_ref, lbp_ref, gpre_ref, win_ref, lng_ref, lnb_ref, wpb_ref,
                       qi_ref, kh_ref, d_ref, i_ref, oin_ref, og_ref, gas_ref, hb_ref, vn_ref):
    n_rows = x_ref.shape[0]
    seq = 4
    x = x_ref[...]
    xn = _rms(x, gpre_ref[...]).astype(BF16)
    lb = _lower_bound(lbp_ref[...])

    def proj(off, width):
        return jnp.dot(xn, win_ref[:, off:off + width], preferred_element_type=F32)

    def tok(width):
        return _mod(_iota((n_rows, width), 0), seq)

    def shift(v, k):
        return pltpu.roll(v, k, 0) if k else v

    qf, logf, kk = _hgrn_features(proj(OFF_Q, D_QA), proj(OFF_F, D_QA), lb)
    iv = proj(OFF_I, D_VA)
    tq = tok(D_QA)
    b = logf + jnp.where(tq >= 1, shift(logf, 1), 0.0)
    b = b + jnp.where(tq >= 2, shift(b, 2), 0.0)
    b_last = jnp.where(tq == seq - 1, b, 0.0)
    b_last = b_last + pltpu.roll(b_last, n_rows - 1, 0)
    b_last = b_last + pltpu.roll(b_last, n_rows - 2, 0)
    qi_ref[...] = qf * jnp.exp(b)
    kh_ref[...] = kk * jnp.exp(b_last - b)
    d_ref[...] = jnp.exp(b_last)
    i_ref[...] = iv

    head_sum = _ones_where(_div(_iota((D_QA, D_VA), 0), DK_A) == _div(_iota((D_QA, D_VA), 1), DV_A))
    tv = tok(D_VA)
    o = jnp.zeros((n_rows, D_VA), F32)
    for k in range(seq):
        pair = qf * shift(kk, k) * jnp.exp(b - shift(b, k))
        sc = jnp.dot(jnp.where(tq >= k, pair, 0.0).astype(BF16), head_sum, preferred_element_type=F32)
        o = o + jnp.where(tv >= k, sc * shift(iv, k), 0.0)
    oin_ref[...] = o

    gu = _gelu(proj(OFF_U, D_B))
    vn = _layer_norm(_gelu(proj(OFF_V, D_B)), lng_ref[...], lnb_ref[...])
    vn_ref[...] = vn
    tg = tok(CH_B)
    obs = []
    for g in range(H_B):
        vg = vn[:, g * CH_B:(g + 1) * CH_B]
        s = jnp.zeros((n_rows, CH_B), F32)
        for t in range(seq):
            s = s + jnp.where(tg == t, bsm_ref[g * seq + t], 0.0)
        for k in range(seq):
            coef = jnp.zeros((n_rows, CH_B), F32)
            for t in range(k, seq):
                coef = coef + jnp.where(tg == t, wsm_ref[(g * seq + t) * seq + (t - k)], 0.0)
            s = s + coef * shift(vg, k)
        obs.append(gu[:, g * CH_B:(g + 1) * CH_B] * s)
    ob = jnp.concatenate(obs, axis=1)
    hb_ref[...] = _sigmoid(proj(OFF_GB, D_MODEL)) * _dot(ob, wpb_ref[...])
    gas_ref[...] = _sigmoid(proj(OFF_GA, D_MODEL))
    og_ref[...] = proj(OFF_OG, D_VA)


def _sample_pre(wsm, bsm, x, lbp, gpre, win, lng, lnb, wpb):
    n_rows = x.shape[0]
    shapes = [(n_rows, D_QA), (n_rows, D_QA), (n_rows, D_QA), (n_rows, D_VA), (n_rows, D_VA),
              (n_rows, D_VA), (n_rows, D_MODEL), (n_rows, D_MODEL), (n_rows, D_B)]
    smem = pl.BlockSpec(memory_space=pltpu.SMEM)
    return pl.pallas_call(
        _sample_pre_kernel,
        in_specs=[smem, smem] + [pl.BlockSpec(memory_space=pltpu.VMEM)] * 7,
        out_shape=tuple(jax.ShapeDtypeStruct(s, F32) for s in shapes),
        compiler_params=pltpu.CompilerParams(vmem_limit_bytes=VMEM_LIMIT),
        name="sample_pre",
    )(wsm, bsm, x, lbp, gpre, win, lng, lnb, wpb)


def _sample_state_kernel(qi_ref, kh_ref, d_ref, i_ref, s_ref, o_ref, sn_ref):
    seq = 4
    nseq = 4
    blk = nseq * seq
    unroll = 2
    rows_seq = _div(_iota((blk, DV_A), 0), seq)

    @pl.loop(0, qi_ref.shape[0] // (blk * unroll))
    def _(trip):
        units = [(u, h) for u in range(unroll) for h in range(H_A)]

        def rows_of(u):
            return pl.ds(pl.multiple_of((trip * unroll + u) * blk, blk), blk)

        def state_index(u, h, bb):
            return ((trip * unroll + u) * nseq + bb) * H_A + h

        def kcols(h):
            return slice(h * DK_A, (h + 1) * DK_A)

        reads = {}
        for u, h in units:
            stacked = jnp.concatenate([s_ref[state_index(u, h, bb)] for bb in range(nseq)], axis=0)
            reads[u, h] = lax.dot_general(qi_ref[rows_of(u), kcols(h)].astype(BF16), stacked.astype(BF16),
                                          (((1,), (1,)), ((), ())), preferred_element_type=F32)
        incs = {}
        for u, h in units:
            i16 = i_ref[rows_of(u), h * DV_A:(h + 1) * DV_A]
            i_sel = jnp.concatenate([jnp.where(rows_seq == bb, i16, 0.0) for bb in range(nseq)], axis=1)
            incs[u, h] = _dot_tn(i_sel, kh_ref[rows_of(u), kcols(h)])
        for u in range(unroll):
            outs = []
            for h in range(H_A):
                r = reads[u, h]
                oh = r[:, :DV_A]
                for bb in range(1, nseq):
                    oh = jnp.where(rows_seq == bb, r[:, bb * DV_A:(bb + 1) * DV_A], oh)
                outs.append(oh)
                d16 = d_ref[rows_of(u), kcols(h)]
                for bb in range(nseq):
                    i = state_index(u, h, bb)
                    sn_ref[i] = s_ref[i] * d16[bb * seq:bb * seq + 1] + incs[u, h][bb * DV_A:(bb + 1) * DV_A]
            o_ref[rows_of(u), :] = jnp.concatenate(outs, axis=1)


def _sample_state(qi, kh, d, iv, state):
    n_rows = qi.shape[0]
    rb = SAMPLE_BB * 4
    sb = SAMPLE_BB * H_A
    return pl.pallas_call(
        _sample_state_kernel,
        grid=(n_rows // rb,),
        in_specs=[pl.BlockSpec((rb, D_QA), lambda i: (i, 0)),
                  pl.BlockSpec((rb, D_QA), lambda i: (i, 0)),
                  pl.BlockSpec((rb, D_QA), lambda i: (i, 0)),
                  pl.BlockSpec((rb, D_VA), lambda i: (i, 0)),
                  pl.BlockSpec((sb, DV_A, DK_A), lambda i: (i, 0, 0))],
        out_specs=(pl.BlockSpec((rb, D_VA), lambda i: (i, 0)),
                   pl.BlockSpec((sb, DV_A, DK_A), lambda i: (i, 0, 0))),
        out_shape=(jax.ShapeDtypeStruct((n_rows, D_VA), F32),
                   jax.ShapeDtypeStruct(state.shape, F32)),
        compiler_params=pltpu.CompilerParams(dimension_semantics=("arbitrary",),
                                             vmem_limit_bytes=VMEM_LIMIT),
        name="sample_state",
    )(qi, kh, d, iv, state)


def _sample_post_kernel(x_ref, oin_ref, oint_ref, og_ref, gas_ref, hb_ref, hng_ref, wpa_ref, wo_ref,
                        gpost_ref, y_ref):
    y_ref[...] = _mix_out(x_ref[...], oin_ref[...] + oint_ref[...], og_ref[...], gas_ref[...], hb_ref[...],
                          hng_ref[...], wpa_ref, wo_ref, gpost_ref[...])


def _sample_post(x, oin, oint, og, gas, hb, hng, wpa, wo, gpost):
    return pl.pallas_call(
        _sample_post_kernel,
        out_shape=jax.ShapeDtypeStruct(x.shape, F32),
        compiler_params=pltpu.CompilerParams(vmem_limit_bytes=VMEM_LIMIT),
        name="sample_post",
    )(x, oin, oint, og, gas, hb, hng, wpa, wo, gpost)


def _sample_ffn_kernel(x_ref, gpre_ref, wu_ref, cache_ref, cw_ref, cb_ref, wdn_ref, gpost_ref, y_ref, cs_ref,
                       xn_ref, acc_ref, old_ref, up_ref, val_ref):
    step = pl.program_id(0)
    nblk = pl.num_programs(0) // 2
    n_rows = x_ref.shape[0]
    n_seq = cache_ref.shape[0]
    seq = n_rows // n_seq
    keep = CONV_W - 1

    @pl.when(step == 0)
    def _():
        xn_ref[...] = _rms(x_ref[...], gpre_ref[...]).astype(BF16)
        acc_ref[...] = jnp.zeros_like(acc_ref)
        old_ref[...] = jnp.zeros_like(old_ref)

    up = jnp.dot(xn_ref[...], wu_ref[...], preferred_element_type=F32)
    up_ref[...] = up
    for b in range(n_seq):
        old_ref[seq * b:seq * b + keep, :] = cache_ref[b]
        cs_ref[b] = up_ref[seq * (b + 1) - keep:seq * (b + 1), :]
    old = old_ref[...]
    tok = _mod(_iota(up.shape, 0), seq)
    m2 = jnp.where(tok >= 2, pltpu.roll(up, 2, 0), old)
    m1 = jnp.where(tok >= 1, pltpu.roll(up, 1, 0), pltpu.roll(old, n_rows - 1, 0))
    conv = cb_ref[...] + cw_ref[0:1, :] * m2 + cw_ref[1:2, :] * m1 + cw_ref[2:3, :] * up

    @pl.when(step < nblk)
    def _():
        val_ref[step] = conv

    @pl.when(step >= nblk)
    def _():
        a = _gelu_mul(conv, val_ref[step - nblk])
        acc_ref[...] += _dot(a, wdn_ref[...])

    @pl.when(step == pl.num_programs(0) - 1)
    def _():
        y_ref[...] = x_ref[...] + _rms(acc_ref[...], gpost_ref[...])


def _sample_ffn(x, gpre, wup, cache, cw, cb, wdn, gpost):
    n_rows = x.shape[0]
    n_seq = cache.shape[0]
    nblk = D_FF // SAMPLE_FFN_CB
    col = lambda i: (0, lax.rem(i + nblk, 2 * nblk))
    col3 = lambda i: (0, 0, lax.rem(i + nblk, 2 * nblk))
    full = lambda shape: pl.BlockSpec(shape, lambda i: (0,) * len(shape))
    return pl.pallas_call(
        _sample_ffn_kernel,
        grid=(2 * nblk,),
        in_specs=[full((n_rows, D_MODEL)), full((1, D_MODEL)),
                  pl.BlockSpec((D_MODEL, SAMPLE_FFN_CB), col),
                  pl.BlockSpec((n_seq, CONV_W - 1, SAMPLE_FFN_CB), col3),
                  pl.BlockSpec((CONV_W, SAMPLE_FFN_CB), col),
                  pl.BlockSpec((1, SAMPLE_FFN_CB), col),
                  pl.BlockSpec((SAMPLE_FFN_CB, D_MODEL), lambda i: (jnp.maximum(i - nblk, 0), 0)),
                  full((1, D_MODEL))],
        out_specs=(full((n_rows, D_MODEL)),
                   pl.BlockSpec((n_seq, CONV_W - 1, SAMPLE_FFN_CB), col3)),
        out_shape=(jax.ShapeDtypeStruct((n_rows, D_MODEL), F32),
                   jax.ShapeDtypeStruct(cache.shape, F32)),
        scratch_shapes=[pltpu.VMEM((n_rows, D_MODEL), BF16), pltpu.VMEM((n_rows, D_MODEL), F32),
                        pltpu.VMEM((n_rows, SAMPLE_FFN_CB), F32), pltpu.VMEM((n_rows, SAMPLE_FFN_CB), F32),
                        pltpu.VMEM((nblk, n_rows, SAMPLE_FFN_CB), F32)],
        compiler_params=pltpu.CompilerParams(dimension_semantics=("arbitrary",),
                                             vmem_limit_bytes=VMEM_LIMIT),
        name="sample_ffn",
    )(x, gpre, wup, cache, cw, cb, wdn, gpost)


def kernel(x_prompt, x_sample, state_hgrn, cache_ffn_conv, lb_param, mix_pre_g, w_in, hgrn_norm_g, gmlp_ln_g, gmlp_ln_b, w_s, b_s, w_pa, w_pb, w_o, mix_post_g, ffn_pre_g, w_up, conv_w, conv_b, w_down, ffn_post_g):
    nb_s, seq_s, _ = x_sample.shape
    row = lambda v: v.reshape(1, -1)
    win, wpa, wpb, wo = (w[0].astype(BF16) for w in (w_in, w_pa, w_pb, w_o))
    ws = w_s[0].astype(BF16)
    gpre, gpost, fpre, fpost = row(mix_pre_g[0]), row(mix_post_g[0]), row(ffn_pre_g[0]), row(ffn_post_g[0])
    lng, lnb, cb = row(gmlp_ln_g[0]), row(gmlp_ln_b[0]), row(conv_b[0])
    hng = row(jnp.tile(hgrn_norm_g[0], H_A))
    bsb = jnp.broadcast_to(b_s[0][:, :, None], (H_B, GMLP_CHUNK, CH_B))

    x1, sp, wup, wdn = _prompt_mixer(x_prompt, lb_param, gpre, win, hng, lng, lnb, ws, bsb, wpa, wpb, wo, gpost,
                                     w_up[0], w_down[0])
    yp, cp = _prompt_ffn(x1, fpre, wup, conv_w[0], cb, wdn, fpost)

    xs = x_sample.reshape(nb_s * seq_s, D_MODEL)
    wsm = w_s[0][:, :seq_s, :seq_s].reshape(-1)
    bsm = b_s[0][:, :seq_s].reshape(-1)
    qi, kh, d, iv, oin, og, gas, hb, vn = _sample_pre(wsm, bsm, xs, lb_param, gpre, win, lng, lnb, wpb)
    st_in = jnp.swapaxes(state_hgrn[0], -1, -2).reshape(nb_s * H_A, DV_A, DK_A)
    oint, ss = _sample_state(qi, kh, d, iv, st_in)
    xs1 = _sample_post(xs, oin, oint, og, gas, hb, hng, wpa, wo, gpost)
    ys, cs = _sample_ffn(xs1, fpre, wup, cache_ffn_conv[0], conv_w[0], cb, wdn, fpost)

    return (yp, ys.reshape(nb_s, seq_s, D_MODEL), jnp.swapaxes(sp, -1, -2)[None],
            jnp.swapaxes(ss.reshape(1, nb_s, H_A, DV_A, DK_A), -1, -2), cp[None], cs[None],
            vn.reshape(1, nb_s, seq_s, D_B))
```

```python
import math

import jax
import jax.numpy as jnp
import numpy as np
from jax import lax
from jax.experimental import pallas as pl
from jax.experimental.pallas import tpu as pltpu

F32 = jnp.float32
BF16 = jnp.bfloat16

D_MODEL = 1024
H_A, DK_A, DV_A = 8, 128, 64
D_QA, D_VA = H_A * DK_A, H_A * DV_A
H_B, CH_B = 4, 128
D_B = H_B * CH_B
GMLP_CHUNK = 128
D_FF = 2816
CONV_W = 3
EPS = 1e-6

OFF_Q, OFF_F, OFF_I, OFF_OG, OFF_U, OFF_V, OFF_GA, OFF_GB = 0, 1024, 2048, 2560, 3072, 3584, 4096, 5120
D_IN = 6144

HGRN_C = 64
PAIR_K = 2 * DK_A
PAIR_V = 2 * DV_A
N_PAIR = H_A // 2
T_MIX = 512
T_FFN = 1024
FFN_CB = 256
SAMPLE_FFN_CB = 1408
SAMPLE_BB = 32
W_CHUNK = 512
VMEM_LIMIT = 56 * 1024 * 1024


def _dot(a, b):
    return jnp.dot(a.astype(BF16), b.astype(BF16), preferred_element_type=F32)


def _dot_nt(a, b):
    return lax.dot_general(a.astype(BF16), b.astype(BF16), (((1,), (1,)), ((), ())),
                           preferred_element_type=F32)


def _dot_tn(a, b):
    return lax.dot_general(a.astype(BF16), b.astype(BF16), (((0,), (0,)), ((), ())),
                           preferred_element_type=F32)


def _dot_split(m, x):
    hi = x.astype(BF16)
    lo = (x - hi.astype(F32)).astype(BF16)
    m = m.astype(BF16)
    return (jnp.dot(m, hi, preferred_element_type=F32) + jnp.dot(m, lo, preferred_element_type=F32))


def _rms(x, g):
    return x * lax.rsqrt(jnp.mean(x * x, axis=-1, keepdims=True) + EPS) * g


_LOG2E = math.log2(math.e)
_GELU_K1 = -2.0 * math.sqrt(2.0 / math.pi) * _LOG2E
_GELU_K3 = _GELU_K1 * 0.044715


def _bf16_split(v):
    hi = float(np.asarray(v, np.float32).astype(jnp.bfloat16))
    return hi, float(np.asarray(v - hi, np.float32).astype(jnp.bfloat16))


def _gelu_gate(x):
    s = x * x
    if x.dtype == F32:
        poly = _GELU_K1 + _GELU_K3 * s
    else:
        (k1h, k1l), (k3h, k3l) = _bf16_split(_GELU_K1), _bf16_split(_GELU_K3)
        poly = (k1h + k3h * s) + (k1l + k3l * s)
    return 1.0 / (1.0 + jnp.exp2(x * poly))


def _gelu(x):
    return x * _gelu_gate(x)


def _gelu_mul(x, v):
    return (x * v) * _gelu_gate(x)


def _layer_norm(x, g, b):
    xc = x - jnp.mean(x, axis=-1, keepdims=True)
    return xc * lax.rsqrt(jnp.mean(xc * xc, axis=-1, keepdims=True) + EPS) * g + b


def _lower_bound(lbp):
    e = jnp.exp(lbp - jnp.max(lbp, axis=0, keepdims=True))
    return e[0:1] / jnp.sum(e, axis=0, keepdims=True)


def _sigmoid(x):
    return 1.0 / (1.0 + jnp.exp2(x * -_LOG2E))


def _exp(x, sign=1.0):
    return jnp.exp2(x * (sign * _LOG2E))


def _hgrn_features(q, f_logit, lb):
    f = lb + (1.0 - lb) * _sigmoid(f_logit)
    return q * _sigmoid(q), jnp.log(f), 1.0 - f


def _iota(shape, dim):
    return lax.broadcasted_iota(jnp.int32, shape, dim)


def _div(x, n):
    assert n & (n - 1) == 0
    return lax.shift_right_logical(x, n.bit_length() - 1)


def _mod(x, n):
    assert n & (n - 1) == 0
    return x & (n - 1)


def _ones_where(cond):
    return jnp.where(cond, 1.0, 0.0).astype(BF16)


def _head_mean_square(o):
    same_head = _ones_where(_div(_iota((D_VA, D_VA), 0), DV_A) == _div(_iota((D_VA, D_VA), 1), DV_A))
    return _dot(o * o, same_head) * (1.0 / DV_A)


def _mix_out(x, o, og, ga_sig, hb, hng, wpa_ref, wo_ref, gpost):
    oa = o * lax.rsqrt(_head_mean_square(o) + EPS) * hng * (og * _sigmoid(og))
    h = ga_sig * _dot(oa, wpa_ref[...]) + hb
    return x + _rms(_dot(h, wo_ref[...]), gpost)


def _prompt_mixer_kernel(x_ref, lbp_ref, gpre_ref, hng_ref, lng_ref, lnb_ref, ws_ref, bsb_ref, gpost_ref,
                         win_hbm, wpa_hbm, wpb_hbm, wo_hbm, wup_ref, wdn_ref,
                         y_ref, sp_ref, wup16_ref, wdn16_ref, win16_ref, wpa16_ref, wpb16_ref, wo16_ref,
                         win_ref, wpa_ref, wpb_ref, wo_ref, stage_ref, sem_ref,
                         st_ref, qt_ref, kt_ref, qi_ref, kh_ref, iv_ref, d_ref, sc_ref, upd_ref, stb_ref,
                         o_ref, ob_ref):
    j = pl.program_id(1)
    step = pl.program_id(0) * pl.num_programs(1) + j

    @pl.when(step == 0)
    def _():
        chunks = [(src, dst, c0)
                  for src, dst in ((win_hbm, win_ref), (wpa_hbm, wpa_ref), (wpb_hbm, wpb_ref), (wo_hbm, wo_ref))
                  for c0 in range(0, src.shape[1], W_CHUNK)]

        def copy(k):
            src, _, c0 = chunks[k]
            return pltpu.make_async_copy(src.at[:, pl.ds(c0, W_CHUNK)],
                                         stage_ref.at[k % 2, pl.ds(0, src.shape[0]), :], sem_ref.at[k % 2])

        copy(0).start()
        for k, (src, dst, c0) in enumerate(chunks):
            if k + 1 < len(chunks):
                copy(k + 1).start()
            copy(k).wait()
            dst[:, c0:c0 + W_CHUNK] = stage_ref[k % 2, 0:src.shape[0], :].astype(BF16)

    for src_ref, out_ref in ((win_ref, win16_ref), (wpa_ref, wpa16_ref), (wpb_ref, wpb16_ref), (wo_ref, wo16_ref)):
        rows = out_ref.shape[0]
        out_ref[...] = src_ref[pl.ds(pl.multiple_of(step * rows, rows), rows), :]
    wup16_ref[...] = wup_ref[...].astype(BF16)
    wdn16_ref[...] = wdn_ref[...].astype(BF16)
    t_tile = x_ref.shape[1]
    c = HGRN_C
    n_chunk = t_tile // c

    @pl.when(j == 0)
    def _():
        st_ref[...] = jnp.zeros_like(st_ref)

    x = x_ref[0]
    xn = _rms(x, gpre_ref[...]).astype(BF16)
    lb = _lower_bound(lbp_ref[...])

    def proj(off, width):
        return jnp.dot(xn, win_ref[:, off:off + width], preferred_element_type=F32)

    q_raw, f_raw = proj(OFF_Q, D_QA), proj(OFF_F, D_QA)
    iv_ref[...] = proj(OFF_I, D_VA).astype(BF16)
    u_raw, v_raw, og_raw = proj(OFF_U, D_B), proj(OFF_V, D_B), proj(OFF_OG, D_VA)
    ga_raw, gb_raw = proj(OFF_GA, D_MODEL), proj(OFF_GB, D_MODEL)
    qf, logf, kk = _hgrn_features(q_raw, f_raw, lb)
    r, s = _iota((c + 8, c), 0), _iota((c + 8, c), 1)
    cum_m = jnp.where(r < c, jnp.where(s <= r, 1.0, 0.0) - jnp.where(s < c // 2, 1.0, 0.0), 1.0)
    for ci in range(n_chunk):
        rows = slice(ci * c, (ci + 1) * c)
        lf = logf[rows]
        cum = _dot_split(cum_m, lf)
        a = cum[:c]
        b_last = cum[c:c + 1]
        b_mid = lf[0:1] - a[0:1]
        qt = qf[rows] * _exp(a)
        kt = kk[rows] * _exp(a, -1.0)
        qt_ref[rows, :] = qt.astype(BF16)
        kt_ref[rows, :] = kt.astype(BF16)
        qi_ref[rows, :] = (qt * jnp.exp(b_mid)).astype(BF16)
        kh_ref[rows, :] = (kt * jnp.exp(b_last - b_mid)).astype(BF16)
        d_ref[ci:ci + 1, :] = jnp.exp(b_last)

    kbd_mask = _div(_iota((2 * c, PAIR_K), 0), c) == _div(_iota((2 * c, PAIR_K), 1), DK_A)
    ibd4_mask = _div(_iota((4 * c, 2 * PAIR_V), 0), c) == _div(_iota((4 * c, 2 * PAIR_V), 1), DV_A)
    sbd_mask = _div(_iota((PAIR_V, PAIR_K), 0), DV_A) == _div(_iota((PAIR_V, PAIR_K), 1), DK_A)
    causal = _mod(_iota((c, 2 * c), 1), c) <= _iota((c, 2 * c), 0)
    zero = jnp.zeros((), BF16)

    for p in range(N_PAIR):
        kcols = slice(p * PAIR_K, (p + 1) * PAIR_K)
        vcols = slice(p * PAIR_V, (p + 1) * PAIR_V)
        for ci in range(n_chunk):
            rows = slice(ci * c, (ci + 1) * c)
            kt = kt_ref[rows, kcols]
            kbd = jnp.where(kbd_mask, jnp.concatenate([kt, kt], axis=0), zero)
            sc = lax.dot_general(qt_ref[rows, kcols], kbd, (((1,), (1,)), ((), ())), preferred_element_type=F32)
            sc_ref[rows, vcols] = jnp.where(causal, sc, 0.0).astype(BF16)
            inc = lax.dot_general(iv_ref[rows, vcols], kh_ref[rows, kcols], (((0,), (0,)), ((), ())),
                                  preferred_element_type=F32)
            upd_ref[p * n_chunk + ci] = jnp.where(sbd_mask, inc, 0.0)

    for p in range(N_PAIR):
        kcols = slice(p * PAIR_K, (p + 1) * PAIR_K)
        st = st_ref[p]
        for ci in range(n_chunk):
            stb_ref[p * n_chunk + ci] = st.astype(BF16)
            st = st * d_ref[ci:ci + 1, kcols] + upd_ref[p * n_chunk + ci]
        st_ref[p] = st

    for q4 in range(N_PAIR // 2):
        vcols4 = slice(2 * q4 * PAIR_V, 2 * (q4 + 1) * PAIR_V)
        for ci in range(n_chunk):
            rows = slice(ci * c, (ci + 1) * c)
            ic4 = iv_ref[rows, vcols4]
            ibd4 = jnp.where(ibd4_mask, jnp.concatenate([ic4] * 4, axis=0), zero)
            inter = [lax.dot_general(qi_ref[rows, p * PAIR_K:(p + 1) * PAIR_K], stb_ref[p * n_chunk + ci],
                                     (((1,), (1,)), ((), ())), preferred_element_type=F32)
                     for p in (2 * q4, 2 * q4 + 1)]
            o_ref[rows, vcols4] = (jnp.dot(sc_ref[rows, vcols4], ibd4, preferred_element_type=F32)
                                   + jnp.concatenate(inter, axis=1))

    gu = _gelu(u_raw)
    vn = _layer_norm(_gelu(v_raw), lng_ref[...], lnb_ref[...]).astype(BF16)
    tril = _iota((GMLP_CHUNK, GMLP_CHUNK), 1) <= _iota((GMLP_CHUNK, GMLP_CHUNK), 0)
    zblk = jnp.zeros((GMLP_CHUNK, CH_B), BF16)
    for g2 in range(H_B // 2):
        g0, g1 = 2 * g2, 2 * g2 + 1
        w2 = jnp.concatenate([jnp.where(tril, ws_ref[g0], zero), jnp.where(tril, ws_ref[g1], zero)], axis=1)
        b2 = jnp.concatenate([bsb_ref[g0], bsb_ref[g1]], axis=1)
        cols2 = slice(g0 * CH_B, (g1 + 1) * CH_B)
        for n in range(t_tile // GMLP_CHUNK):
            rows = slice(n * GMLP_CHUNK, (n + 1) * GMLP_CHUNK)
            v2 = vn[rows, cols2]
            vbd = jnp.concatenate([jnp.concatenate([v2[:, :CH_B], zblk], axis=1),
                                   jnp.concatenate([zblk, v2[:, CH_B:]], axis=1)], axis=0)
            ob_ref[rows, cols2] = gu[rows, cols2] * (jnp.dot(w2, vbd, preferred_element_type=F32) + b2)

    hb = _sigmoid(gb_raw) * _dot(ob_ref[...], wpb_ref[...])
    y_ref[0] = _mix_out(x, o_ref[...], og_raw, _sigmoid(ga_raw), hb,
                        hng_ref[...], wpa_ref, wo_ref, gpost_ref[...])

    @pl.when(j == pl.num_programs(1) - 1)
    def _():
        for h in range(H_A):
            hh = h % 2
            sp_ref[0, h] = st_ref[h // 2, hh * DV_A:(hh + 1) * DV_A, hh * DK_A:(hh + 1) * DK_A]


def _const_spec(shape):
    return pl.BlockSpec(shape, lambda *_: (0,) * len(shape), pipeline_mode=pl.Buffered(1))


def _prompt_mixer(x, lbp, gpre, hng, lng, lnb, ws, bsb, gpost, win, wpa, wpb, wo, wup, wdn):
    nb, seq, _ = x.shape
    n_j = seq // T_MIX
    n_step = nb * n_j
    grid = (nb, n_j)
    n_pc = N_PAIR * (T_MIX // HGRN_C)
    consts = (lbp, gpre, hng, lng, lnb, ws, bsb, gpost)
    own = (win, wpa, wpb, wo)
    assert all(w.shape[1] % W_CHUNK == 0 and w.shape[0] <= D_MODEL for w in own)

    def slab_spec(w, steps_per_slab=1):
        rows = w.shape[0] * steps_per_slab // n_step
        assert rows * n_step == w.shape[0] * steps_per_slab and rows % 16 == 0
        return pl.BlockSpec((rows, w.shape[1]), lambda b, j: ((b * n_j + j) // steps_per_slab, 0))

    up_spec, dn_spec = slab_spec(wup), slab_spec(wdn, 2)
    own_specs = [slab_spec(w) for w in own]
    return pl.pallas_call(
        _prompt_mixer_kernel,
        grid=grid,
        in_specs=[pl.BlockSpec((1, T_MIX, D_MODEL), lambda b, j: (b, j, 0))]
        + [_const_spec(w.shape) for w in consts] + [pl.BlockSpec(memory_space=pl.ANY)] * len(own)
        + [up_spec, dn_spec],
        out_specs=(pl.BlockSpec((1, T_MIX, D_MODEL), lambda b, j: (b, j, 0)),
                   pl.BlockSpec((1, H_A, DV_A, DK_A), lambda b, j: (b, 0, 0, 0)),
                   up_spec, dn_spec, *own_specs),
        out_shape=(jax.ShapeDtypeStruct(x.shape, F32),
                   jax.ShapeDtypeStruct((nb, H_A, DV_A, DK_A), F32),
                   jax.ShapeDtypeStruct(wup.shape, BF16),
                   jax.ShapeDtypeStruct(wdn.shape, BF16),
                   *(jax.ShapeDtypeStruct(w.shape, BF16) for w in own)),
        scratch_shapes=[*(pltpu.VMEM(w.shape, BF16) for w in own),
                        pltpu.VMEM((2, D_MODEL, W_CHUNK), F32),
                        pltpu.SemaphoreType.DMA((2,)),
                        pltpu.VMEM((N_PAIR, PAIR_V, PAIR_K), F32),
                        pltpu.VMEM((T_MIX, D_QA), BF16),
                        pltpu.VMEM((T_MIX, D_QA), BF16),
                        pltpu.VMEM((T_MIX, D_QA), BF16),
                        pltpu.VMEM((T_MIX, D_QA), BF16),
                        pltpu.VMEM((T_MIX, D_VA), BF16),
                        pltpu.VMEM((T_MIX // HGRN_C, D_QA), F32),
                        pltpu.VMEM((T_MIX, D_VA), BF16),
                        pltpu.VMEM((n_pc, PAIR_V, PAIR_K), F32),
                        pltpu.VMEM((n_pc, PAIR_V, PAIR_K), BF16),
                        pltpu.VMEM((T_MIX, D_VA), F32),
                        pltpu.VMEM((T_MIX, D_B), F32)],
        compiler_params=pltpu.CompilerParams(dimension_semantics=("arbitrary", "arbitrary"),
                                             vmem_limit_bytes=VMEM_LIMIT),
        name="prompt_mixer",
    )(x, *consts, *own, wup, wdn)


def _prompt_ffn_kernel(x_ref, gpre_ref, wup_ref, cw_ref, cb_ref, wdn_ref, gpost_ref, y_ref, cp_ref,
                       tail_ref, a_ref):
    j = pl.program_id(1)
    t_tile = x_ref.shape[1]

    @pl.when(j == 0)
    def _():
        tail_ref[...] = jnp.zeros_like(tail_ref)

    x = x_ref[0]
    xn = _rms(x, gpre_ref[...]).astype(BF16)
    row = _iota((t_tile, FFN_CB), 0)

    def up_proj(c0):
        return jnp.dot(xn, wup_ref[:, c0:c0 + FFN_CB], preferred_element_type=F32)

    def conv(c0, up):
        cols = slice(c0, c0 + FFN_CB)
        p0 = tail_ref[6:7, cols]
        p1 = tail_ref[7:8, cols]
        tail_ref[:, cols] = up[t_tile - 8:]
        m1 = jnp.where(row == 0, p1, pltpu.roll(up, 1, 0))
        m2 = jnp.where(row == 0, p0, jnp.where(row == 1, p1, pltpu.roll(up, 2, 0)))
        w0, w1, w2 = (cw_ref[k:k + 1, cols].astype(BF16) for k in range(CONV_W))
        return (cb_ref[:, cols].astype(BF16) + w0 * m2.astype(BF16) + w1 * m1.astype(BF16)
                + w2 * up.astype(BF16))

    nblk = D_FF // FFN_CB
    ups = (up_proj(0), up_proj(D_FF))
    for blk in range(nblk):
        c0 = blk * FFN_CB
        nxt = (up_proj(c0 + FFN_CB), up_proj(D_FF + c0 + FFN_CB)) if blk + 1 < nblk else None
        a_ref[:, c0:c0 + FFN_CB] = _gelu_mul(conv(c0, ups[0]), conv(D_FF + c0, ups[1])).astype(BF16)
        ups = nxt

    y_ref[0] = x + _rms(jnp.dot(a_ref[...], wdn_ref[...], preferred_element_type=F32), gpost_ref[...])

    @pl.when(j == pl.num_programs(1) - 1)
    def _():
        cp_ref[0] = tail_ref[6:8, :]


def _prompt_ffn(x, gpre, wup, cw, cb, wdn, gpost):
    nb, seq, _ = x.shape
    weights = (gpre, wup, cw, cb, wdn, gpost)
    return pl.pallas_call(
        _prompt_ffn_kernel,
        grid=(nb, seq // T_FFN),
        in_specs=[pl.BlockSpec((1, T_FFN, D_MODEL), lambda b, j: (b, j, 0))]
        + [_const_spec(w.shape) for w in weights],
        out_specs=(pl.BlockSpec((1, T_FFN, D_MODEL), lambda b, j: (b, j, 0)),
                   pl.BlockSpec((1, CONV_W - 1, 2 * D_FF), lambda b, j: (b, 0, 0))),
        out_shape=(jax.ShapeDtypeStruct(x.shape, F32),
                   jax.ShapeDtypeStruct((nb, CONV_W - 1, 2 * D_FF), F32)),
        scratch_shapes=[pltpu.VMEM((8, 2 * D_FF), F32),
                        pltpu.VMEM((T_FFN, D_FF), BF16)],
        compiler_params=pltpu.CompilerParams(dimension_semantics=("arbitrary", "arbitrary"),
                                             vmem_limit_bytes=VMEM_LIMIT),
        name="prompt_ffn",
    )(x, *weights)


def _sample_pre_kernel(wsm_ref, bsm_ref, x_ref, lbp_ref, gpre_ref, win_ref, lng_ref, lnb_ref, wpb_ref,
                       qi_ref, kh_ref, d_ref, i_ref, oin_ref, og_ref, gas_ref, hb_ref, vn_ref):
    n_rows = x_ref.shape[0]
    seq = 4
    x = x_ref[...]
    xn = _rms(x, gpre_ref[...]).astype(BF16)
    lb = _lower_bound(lbp_ref[...])

    def proj(off, width):
        return jnp.dot(xn, win_ref[:, off:off + width], preferred_element_type=F32)

    def tok(width):
        return _mod(_iota((n_rows, width), 0), seq)

    def shift(v, k):
        return pltpu.roll(v, k, 0) if k else v

    qf, logf, kk = _hgrn_features(proj(OFF_Q, D_QA), proj(OFF_F, D_QA), lb)
    iv = proj(OFF_I, D_VA)
    tq = tok(D_QA)
    b = logf + jnp.where(tq >= 1, shift(logf, 1), 0.0)
    b = b + jnp.where(tq >= 2, shift(b, 2), 0.0)
    b_last = jnp.where(tq == seq - 1, b, 0.0)
    b_last = b_last + pltpu.roll(b_last, n_rows - 1, 0)
    b_last = b_last + pltpu.roll(b_last, n_rows - 2, 0)
    qi_ref[...] = qf * jnp.exp(b)
    kh_ref[...] = kk * jnp.exp(b_last - b)
    d_ref[...] = jnp.exp(b_last)
    i_ref[...] = iv

    head_sum = _ones_where(_div(_iota((D_QA, D_VA), 0), DK_A) == _div(_iota((D_QA, D_VA), 1), DV_A))
    tv = tok(D_VA)
    o = jnp.zeros((n_rows, D_VA), F32)
    for k in range(seq):
        pair = qf * shift(kk, k) * jnp.exp(b - shift(b, k))
        sc = jnp.dot(jnp.where(tq >= k, pair, 0.0).astype(BF16), head_sum, preferred_element_type=F32)
        o = o + jnp.where(tv >= k, sc * shift(iv, k), 0.0)
    oin_ref[...] = o

    gu = _gelu(proj(OFF_U, D_B))
    vn = _layer_norm(_gelu(proj(OFF_V, D_B)), lng_ref[...], lnb_ref[...])
    vn_ref[...] = vn
    tg = tok(CH_B)
    obs = []
    for g in range(H_B):
        vg = vn[:, g * CH_B:(g + 1) * CH_B]
        s = jnp.zeros((n_rows, CH_B), F32)
        for t in range(seq):
            s = s + jnp.where(tg == t, bsm_ref[g * seq + t], 0.0)
        for k in range(seq):
            coef = jnp.zeros((n_rows, CH_B), F32)
            for t in range(k, seq):
                coef = coef + jnp.where(tg == t, wsm_ref[(g * seq + t) * seq + (t - k)], 0.0)
            s = s + coef * shift(vg, k)
        obs.append(gu[:, g * CH_B:(g + 1) * CH_B] * s)
    ob = jnp.concatenate(obs, axis=1)
    hb_ref[...] = _sigmoid(proj(OFF_GB, D_MODEL)) * _dot(ob, wpb_ref[...])
    gas_ref[...] = _sigmoid(proj(OFF_GA, D_MODEL))
    og_ref[...] = proj(OFF_OG, D_VA)


def _sample_pre(wsm, bsm, x, lbp, gpre, win, lng, lnb, wpb):
    n_rows = x.shape[0]
    shapes = [(n_rows, D_QA), (n_rows, D_QA), (n_rows, D_QA), (n_rows, D_VA), (n_rows, D_VA),
              (n_rows, D_VA), (n_rows, D_MODEL), (n_rows, D_MODEL), (n_rows, D_B)]
    smem = pl.BlockSpec(memory_space=pltpu.SMEM)
    return pl.pallas_call(
        _sample_pre_kernel,
        in_specs=[smem, smem] + [pl.BlockSpec(memory_space=pltpu.VMEM)] * 7,
        out_shape=tuple(jax.ShapeDtypeStruct(s, F32) for s in shapes),
        compiler_params=pltpu.CompilerParams(vmem_limit_bytes=VMEM_LIMIT),
        name="sample_pre",
    )(wsm, bsm, x, lbp, gpre, win, lng, lnb, wpb)


def _sample_state_kernel(qi_ref, kh_ref, d_ref, i_ref, s_ref, o_ref, sn_ref):
    seq = 4
    nseq = 4
    blk = nseq * seq
    unroll = 2
    rows_seq = _div(_iota((blk, DV_A), 0), seq)

    @pl.loop(0, qi_ref.shape[0] // (blk * unroll))
    def _(trip):
        units = [(u, h) for u in range(unroll) for h in range(H_A)]

        def rows_of(u):
            return pl.ds(pl.multiple_of((trip * unroll + u) * blk, blk), blk)

        def state_index(u, h, bb):
            return ((trip * unroll + u) * nseq + bb) * H_A + h

        def kcols(h):
            return slice(h * DK_A, (h + 1) * DK_A)

        reads = {}
        for u, h in units:
            stacked = jnp.concatenate([s_ref[state_index(u, h, bb)] for bb in range(nseq)], axis=0)
            reads[u, h] = lax.dot_general(qi_ref[rows_of(u), kcols(h)].astype(BF16), stacked.astype(BF16),
                                          (((1,), (1,)), ((), ())), preferred_element_type=F32)
        incs = {}
        for u, h in units:
            i16 = i_ref[rows_of(u), h * DV_A:(h + 1) * DV_A]
            i_sel = jnp.concatenate([jnp.where(rows_seq == bb, i16, 0.0) for bb in range(nseq)], axis=1)
            incs[u, h] = _dot_tn(i_sel, kh_ref[rows_of(u), kcols(h)])
        for u in range(unroll):
            outs = []
            for h in range(H_A):
                r = reads[u, h]
                oh = r[:, :DV_A]
                for bb in range(1, nseq):
                    oh = jnp.where(rows_seq == bb, r[:, bb * DV_A:(bb + 1) * DV_A], oh)
                outs.append(oh)
                d16 = d_ref[rows_of(u), kcols(h)]
                for bb in range(nseq):
                    i = state_index(u, h, bb)
                    sn_ref[i] = s_ref[i] * d16[bb * seq:bb * seq + 1] + incs[u, h][bb * DV_A:(bb + 1) * DV_A]
            o_ref[rows_of(u), :] = jnp.concatenate(outs, axis=1)


def _sample_state(qi, kh, d, iv, state):
    n_rows = qi.shape[0]
    rb = SAMPLE_BB * 4
    sb = SAMPLE_BB * H_A
    return pl.pallas_call(
        _sample_state_kernel,
        grid=(n_rows // rb,),
        in_specs=[pl.BlockSpec((rb, D_QA), lambda i: (i, 0)),
                  pl.BlockSpec((rb, D_QA), lambda i: (i, 0)),
                  pl.BlockSpec((rb, D_QA), lambda i: (i, 0)),
                  pl.BlockSpec((rb, D_VA), lambda i: (i, 0)),
                  pl.BlockSpec((sb, DV_A, DK_A), lambda i: (i, 0, 0))],
        out_specs=(pl.BlockSpec((rb, D_VA), lambda i: (i, 0)),
                   pl.BlockSpec((sb, DV_A, DK_A), lambda i: (i, 0, 0))),
        out_shape=(jax.ShapeDtypeStruct((n_rows, D_VA), F32),
                   jax.ShapeDtypeStruct(state.shape, F32)),
        compiler_params=pltpu.CompilerParams(dimension_semantics=("arbitrary",),
                                             vmem_limit_bytes=VMEM_LIMIT),
        name="sample_state",
    )(qi, kh, d, iv, state)


def _sample_post_kernel(x_ref, oin_ref, oint_ref, og_ref, gas_ref, hb_ref, hng_ref, wpa_ref, wo_ref,
                        gpost_ref, y_ref):
    y_ref[...] = _mix_out(x_ref[...], oin_ref[...] + oint_ref[...], og_ref[...], gas_ref[...], hb_ref[...],
                          hng_ref[...], wpa_ref, wo_ref, gpost_ref[...])


def _sample_post(x, oin, oint, og, gas, hb, hng, wpa, wo, gpost):
    return pl.pallas_call(
        _sample_post_kernel,
        out_shape=jax.ShapeDtypeStruct(x.shape, F32),
        compiler_params=pltpu.CompilerParams(vmem_limit_bytes=VMEM_LIMIT),
        name="sample_post",
    )(x, oin, oint, og, gas, hb, hng, wpa, wo, gpost)


def _sample_ffn_kernel(x_ref, gpre_ref, wu_ref, cache_ref, cw_ref, cb_ref, wdn_ref, gpost_ref, y_ref, cs_ref,
                       xn_ref, acc_ref, old_ref, up_ref, val_ref):
    step = pl.program_id(0)
    nblk = pl.num_programs(0) // 2
    n_rows = x_ref.shape[0]
    n_seq = cache_ref.shape[0]
    seq = n_rows // n_seq
    keep = CONV_W - 1

    @pl.when(step == 0)
    def _():
        xn_ref[...] = _rms(x_ref[...], gpre_ref[...]).astype(BF16)
        acc_ref[...] = jnp.zeros_like(acc_ref)
        old_ref[...] = jnp.zeros_like(old_ref)

    up = jnp.dot(xn_ref[...], wu_ref[...], preferred_element_type=F32)
    up_ref[...] = up
    for b in range(n_seq):
        old_ref[seq * b:seq * b + keep, :] = cache_ref[b]
        cs_ref[b] = up_ref[seq * (b + 1) - keep:seq * (b + 1), :]
    old = old_ref[...]
    tok = _mod(_iota(up.shape, 0), seq)
    m2 = jnp.where(tok >= 2, pltpu.roll(up, 2, 0), old)
    m1 = jnp.where(tok >= 1, pltpu.roll(up, 1, 0), pltpu.roll(old, n_rows - 1, 0))
    conv = cb_ref[...] + cw_ref[0:1, :] * m2 + cw_ref[1:2, :] * m1 + cw_ref[2:3, :] * up

    @pl.when(step < nblk)
    def _():
        val_ref[step] = conv

    @pl.when(step >= nblk)
    def _():
        a = _gelu_mul(conv, val_ref[step - nblk])
        acc_ref[...] += _dot(a, wdn_ref[...])

    @pl.when(step == pl.num_programs(0) - 1)
    def _():
        y_ref[...] = x_ref[...] + _rms(acc_ref[...], gpost_ref[...])


def _sample_ffn(x, gpre, wup, cache, cw, cb, wdn, gpost):
    n_rows = x.shape[0]
    n_seq = cache.shape[0]
    nblk = D_FF // SAMPLE_FFN_CB
    col = lambda i: (0, lax.rem(i + nblk, 2 * nblk))
    col3 = lambda i: (0, 0, lax.rem(i + nblk, 2 * nblk))
    full = lambda shape: pl.BlockSpec(shape, lambda i: (0,) * len(shape))
    return pl.pallas_call(
        _sample_ffn_kernel,
        grid=(2 * nblk,),
        in_specs=[full((n_rows, D_MODEL)), full((1, D_MODEL)),
                  pl.BlockSpec((D_MODEL, SAMPLE_FFN_CB), col),
                  pl.BlockSpec((n_seq, CONV_W - 1, SAMPLE_FFN_CB), col3),
                  pl.BlockSpec((CONV_W, SAMPLE_FFN_CB), col),
                  pl.BlockSpec((1, SAMPLE_FFN_CB), col),
                  pl.BlockSpec((SAMPLE_FFN_CB, D_MODEL), lambda i: (jnp.maximum(i - nblk, 0), 0)),
                  full((1, D_MODEL))],
        out_specs=(full((n_rows, D_MODEL)),
                   pl.BlockSpec((n_seq, CONV_W - 1, SAMPLE_FFN_CB), col3)),
        out_shape=(jax.ShapeDtypeStruct((n_rows, D_MODEL), F32),
                   jax.ShapeDtypeStruct(cache.shape, F32)),
        scratch_shapes=[pltpu.VMEM((n_rows, D_MODEL), BF16), pltpu.VMEM((n_rows, D_MODEL), F32),
                        pltpu.VMEM((n_rows, SAMPLE_FFN_CB), F32), pltpu.VMEM((n_rows, SAMPLE_FFN_CB), F32),
                        pltpu.VMEM((nblk, n_rows, SAMPLE_FFN_CB), F32)],
        compiler_params=pltpu.CompilerParams(dimension_semantics=("arbitrary",),
                                             vmem_limit_bytes=VMEM_LIMIT),
        name="sample_ffn",
    )(x, gpre, wup, cache, cw, cb, wdn, gpost)


def kernel(x_prompt, x_sample, state_hgrn, cache_ffn_conv, lb_param, mix_pre_g, w_in, hgrn_norm_g, gmlp_ln_g, gmlp_ln_b, w_s, b_s, w_pa, w_pb, w_o, mix_post_g, ffn_pre_g, w_up, conv_w, conv_b, w_down, ffn_post_g):
    nb_s, seq_s, _ = x_sample.shape
    row = lambda v: v.reshape(1, -1)
    ws = w_s[0].astype(BF16)
    gpre, gpost, fpre, fpost = row(mix_pre_g[0]), row(mix_post_g[0]), row(ffn_pre_g[0]), row(ffn_post_g[0])
    lng, lnb, cb = row(gmlp_ln_g[0]), row(gmlp_ln_b[0]), row(conv_b[0])
    hng = row(jnp.tile(hgrn_norm_g[0], H_A))
    bsb = jnp.broadcast_to(b_s[0][:, :, None], (H_B, GMLP_CHUNK, CH_B))

    x1, sp, wup, wdn, win, wpa, wpb, wo = _prompt_mixer(
        x_prompt, lb_param, gpre, hng, lng, lnb, ws, bsb, gpost, w_in[0], w_pa[0], w_pb[0], w_o[0], w_up[0], w_down[0])
    yp, cp = _prompt_ffn(x1, fpre, wup, conv_w[0], cb, wdn, fpost)

    xs = x_sample.reshape(nb_s * seq_s, D_MODEL)
    wsm = w_s[0][:, :seq_s, :seq_s].reshape(-1)
    bsm = b_s[0][:, :seq_s].reshape(-1)
    qi, kh, d, iv, oin, og, gas, hb, vn = _sample_pre(wsm, bsm, xs, lb_param, gpre, win, lng, lnb, wpb)
    st_in = jnp.swapaxes(state_hgrn[0], -1, -2).reshape(nb_s * H_A, DV_A, DK_A)
    oint, ss = _sample_state(qi, kh, d, iv, st_in)
    xs1 = _sample_post(xs, oin, oint, og, gas, hb, hng, wpa, wo, gpost)
    ys, cs = _sample_ffn(xs1, fpre, wup, cache_ffn_conv[0], conv_w[0], cb, wdn, fpost)

    return (yp, ys.reshape(nb_s, seq_s, D_MODEL), jnp.swapaxes(sp, -1, -2)[None],
            jnp.swapaxes(ss.reshape(1, nb_s, H_A, DV_A, DK_A), -1, -2), cp[None], cs[None],
            vn.reshape(1, nb_s, seq_s, D_B))
```

```python
import math

import jax
import jax.numpy as jnp
import numpy as np
from jax import lax
from jax.experimental import pallas as pl
from jax.experimental.pallas import tpu as pltpu

F32 = jnp.float32
BF16 = jnp.bfloat16

D_MODEL = 1024
H_A, DK_A, DV_A = 8, 128, 64
D_QA, D_VA = H_A * DK_A, H_A * DV_A
H_B, CH_B = 4, 128
D_B = H_B * CH_B
GMLP_CHUNK = 128
D_FF = 2816
CONV_W = 3
EPS = 1e-6

OFF_Q, OFF_F, OFF_I, OFF_OG, OFF_U, OFF_V, OFF_GA, OFF_GB = 0, 1024, 2048, 2560, 3072, 3584, 4096, 5120
D_IN = 6144

HGRN_C = 64
PAIR_K = 2 * DK_A
PAIR_V = 2 * DV_A
N_PAIR = H_A // 2
T_MIX = 512
T_FFN = 1024
FFN_CB = 256
SAMPLE_FFN_CB = 1408
SAMPLE_BB = 32
W_CHUNK = 512
VMEM_LIMIT = 56 * 1024 * 1024


def _dot(a, b):
    return jnp.dot(a.astype(BF16), b.astype(BF16), preferred_element_type=F32)


def _dot_nt(a, b):
    return lax.dot_general(a.astype(BF16), b.astype(BF16), (((1,), (1,)), ((), ())),
                           preferred_element_type=F32)


def _dot_tn(a, b):
    return lax.dot_general(a.astype(BF16), b.astype(BF16), (((0,), (0,)), ((), ())),
                           preferred_element_type=F32)


def _dot_split(m, x):
    hi = x.astype(BF16)
    lo = (x - hi.astype(F32)).astype(BF16)
    m = m.astype(BF16)
    return (jnp.dot(m, hi, preferred_element_type=F32) + jnp.dot(m, lo, preferred_element_type=F32))


def _rms(x, g):
    return x * lax.rsqrt(jnp.mean(x * x, axis=-1, keepdims=True) + EPS) * g


_LOG2E = math.log2(math.e)
_GELU_K1 = -2.0 * math.sqrt(2.0 / math.pi) * _LOG2E
_GELU_K3 = _GELU_K1 * 0.044715


def _bf16_split(v):
    hi = float(np.asarray(v, np.float32).astype(jnp.bfloat16))
    return hi, float(np.asarray(v - hi, np.float32).astype(jnp.bfloat16))


def _gelu_gate(x):
    s = x * x
    if x.dtype == F32:
        poly = _GELU_K1 + _GELU_K3 * s
    else:
        (k1h, k1l), (k3h, k3l) = _bf16_split(_GELU_K1), _bf16_split(_GELU_K3)
        poly = (k1h + k3h * s) + (k1l + k3l * s)
    return 1.0 / (1.0 + jnp.exp2(x * poly))


def _gelu(x):
    return x * _gelu_gate(x)


def _gelu_mul(x, v):
    return (x * v) * _gelu_gate(x)


def _layer_norm(x, g, b):
    xc = x - jnp.mean(x, axis=-1, keepdims=True)
    return xc * lax.rsqrt(jnp.mean(xc * xc, axis=-1, keepdims=True) + EPS) * g + b


def _lower_bound(lbp):
    e = jnp.exp(lbp - jnp.max(lbp, axis=0, keepdims=True))
    return e[0:1] / jnp.sum(e, axis=0, keepdims=True)


def _sigmoid(x):
    return 1.0 / (1.0 + jnp.exp2(x * -_LOG2E))


def _exp(x, sign=1.0):
    return jnp.exp2(x * (sign * _LOG2E))


def _hgrn_features(q, f_logit, lb):
    f = lb + (1.0 - lb) * _sigmoid(f_logit)
    return q * _sigmoid(q), jnp.log(f), 1.0 - f


def _iota(shape, dim):
    return lax.broadcasted_iota(jnp.int32, shape, dim)


def _div(x, n):
    assert n & (n - 1) == 0
    return lax.shift_right_logical(x, n.bit_length() - 1)


def _mod(x, n):
    assert n & (n - 1) == 0
    return x & (n - 1)


def _ones_where(cond):
    return jnp.where(cond, 1.0, 0.0).astype(BF16)


def _head_mean_square(o):
    same_head = _ones_where(_div(_iota((D_VA, D_VA), 0), DV_A) == _div(_iota((D_VA, D_VA), 1), DV_A))
    return _dot(o * o, same_head) * (1.0 / DV_A)


def _mix_out(x, o, og, ga_sig, hb, hng, wpa_ref, wo_ref, gpost):
    oa = o * lax.rsqrt(_head_mean_square(o) + EPS) * hng * (og * _sigmoid(og))
    h = ga_sig * _dot(oa, wpa_ref[...]) + hb
    return x + _rms(_dot(h, wo_ref[...]), gpost)


def _prompt_mixer_kernel(x_ref, lbp_ref, gpre_ref, hng_ref, lng_ref, lnb_ref, ws_ref, bsb_ref, gpost_ref,
                         win_hbm, wpa_hbm, wpb_hbm, wo_hbm, wup_ref, wdn_ref,
                         y_ref, sp_ref, wup16_ref, wdn16_ref, win16_ref, wpa16_ref, wpb16_ref, wo16_ref,
                         win_ref, wpa_ref, wpb_ref, wo_ref, stage_ref, sem_ref,
                         st_ref, qt_ref, kt_ref, qi_ref, kh_ref, iv_ref, d_ref, sc_ref, upd_ref, stb_ref,
                         o_ref, ob_ref):
    j = pl.program_id(1)
    step = pl.program_id(0) * pl.num_programs(1) + j

    @pl.when(step == 0)
    def _():
        chunks = [(src, dst, c0)
                  for src, dst in ((win_hbm, win_ref), (wpa_hbm, wpa_ref), (wpb_hbm, wpb_ref), (wo_hbm, wo_ref))
                  for c0 in range(0, src.shape[1], W_CHUNK)]

        def copy(k):
            src, _, c0 = chunks[k]
            return pltpu.make_async_copy(src.at[:, pl.ds(c0, W_CHUNK)],
                                         stage_ref.at[k % 2, pl.ds(0, src.shape[0]), :], sem_ref.at[k % 2])

        copy(0).start()
        for k, (src, dst, c0) in enumerate(chunks):
            if k + 1 < len(chunks):
                copy(k + 1).start()
            copy(k).wait()
            dst[:, c0:c0 + W_CHUNK] = stage_ref[k % 2, 0:src.shape[0], :].astype(BF16)

    for src_ref, out_ref in ((win_ref, win16_ref), (wpa_ref, wpa16_ref), (wpb_ref, wpb16_ref), (wo_ref, wo16_ref)):
        rows = out_ref.shape[0]
        out_ref[...] = src_ref[pl.ds(pl.multiple_of(step * rows, rows), rows), :]
    wup16_ref[...] = wup_ref[...].astype(BF16)
    wdn16_ref[...] = wdn_ref[...].astype(BF16)
    t_tile = x_ref.shape[1]
    c = HGRN_C
    n_chunk = t_tile // c

    @pl.when(j == 0)
    def _():
        st_ref[...] = jnp.zeros_like(st_ref)

    x = x_ref[0]
    xn = _rms(x, gpre_ref[...]).astype(BF16)
    lb = _lower_bound(lbp_ref[...])

    def proj(off, width):
        return jnp.dot(xn, win_ref[:, off:off + width], preferred_element_type=F32)

    q_raw, f_raw = proj(OFF_Q, D_QA), proj(OFF_F, D_QA)
    iv_ref[...] = proj(OFF_I, D_VA).astype(BF16)
    u_raw, v_raw, og_raw = proj(OFF_U, D_B), proj(OFF_V, D_B), proj(OFF_OG, D_VA)
    ga_raw, gb_raw = proj(OFF_GA, D_MODEL), proj(OFF_GB, D_MODEL)
    qf, logf, kk = _hgrn_features(q_raw, f_raw, lb)
    r, s = _iota((c + 8, c), 0), _iota((c + 8, c), 1)
    cum_m = jnp.where(r < c, jnp.where(s <= r, 1.0, 0.0) - jnp.where(s < c // 2, 1.0, 0.0), 1.0)
    for ci in range(n_chunk):
        rows = slice(ci * c, (ci + 1) * c)
        lf = logf[rows]
        cum = _dot_split(cum_m, lf)
        a = cum[:c]
        b_last = cum[c:c + 1]
        b_mid = lf[0:1] - a[0:1]
        qt = qf[rows] * _exp(a)
        kt = kk[rows] * _exp(a, -1.0)
        qt_ref[rows, :] = qt.astype(BF16)
        kt_ref[rows, :] = kt.astype(BF16)
        qi_ref[rows, :] = (qt * jnp.exp(b_mid)).astype(BF16)
        kh_ref[rows, :] = (kt * jnp.exp(b_last - b_mid)).astype(BF16)
        d_ref[ci:ci + 1, :] = jnp.exp(b_last)

    kbd_t_mask = _div(_iota((PAIR_K, 2 * c), 0), DK_A) == _div(_iota((PAIR_K, 2 * c), 1), c)
    ibd4_mask = _div(_iota((4 * c, 2 * PAIR_V), 0), c) == _div(_iota((4 * c, 2 * PAIR_V), 1), DV_A)
    sbd_mask = _div(_iota((PAIR_V, PAIR_K), 0), DV_A) == _div(_iota((PAIR_V, PAIR_K), 1), DK_A)
    causal = _mod(_iota((c, 2 * c), 1), c) <= _iota((c, 2 * c), 0)
    zero = jnp.zeros((), BF16)

    for p in range(N_PAIR):
        kcols = slice(p * PAIR_K, (p + 1) * PAIR_K)
        vcols = slice(p * PAIR_V, (p + 1) * PAIR_V)
        for ci in range(n_chunk):
            rows = slice(ci * c, (ci + 1) * c)
            kt = kt_ref[rows, kcols]
            kt_t = kt.T
            kbd_t = jnp.where(kbd_t_mask, jnp.concatenate([kt_t, kt_t], axis=1), zero)
            sc = jnp.dot(qt_ref[rows, kcols], kbd_t, preferred_element_type=F32)
            sc_ref[rows, vcols] = jnp.where(causal, sc, 0.0).astype(BF16)
            inc = lax.dot_general(iv_ref[rows, vcols], kh_ref[rows, kcols], (((0,), (0,)), ((), ())),
                                  preferred_element_type=F32)
            upd_ref[p * n_chunk + ci] = jnp.where(sbd_mask, inc, 0.0)

    for p in range(N_PAIR):
        kcols = slice(p * PAIR_K, (p + 1) * PAIR_K)
        st = st_ref[p]
        for ci in range(n_chunk):
            stb_ref[p * n_chunk + ci] = st.T.astype(BF16)
            st = st * d_ref[ci:ci + 1, kcols] + upd_ref[p * n_chunk + ci]
        st_ref[p] = st

    for q4 in range(N_PAIR // 2):
        vcols4 = slice(2 * q4 * PAIR_V, 2 * (q4 + 1) * PAIR_V)
        for ci in range(n_chunk):
            rows = slice(ci * c, (ci + 1) * c)
            ic4 = iv_ref[rows, vcols4]
            ibd4 = jnp.where(ibd4_mask, jnp.concatenate([ic4] * 4, axis=0), zero)
            inter = [jnp.dot(qi_ref[rows, p * PAIR_K:(p + 1) * PAIR_K], stb_ref[p * n_chunk + ci],
                             preferred_element_type=F32)
                     for p in (2 * q4, 2 * q4 + 1)]
            o_ref[rows, vcols4] = (jnp.dot(sc_ref[rows, vcols4], ibd4, preferred_element_type=F32)
                                   + jnp.concatenate(inter, axis=1))

    gu = _gelu(u_raw)
    vn = _layer_norm(_gelu(v_raw), lng_ref[...], lnb_ref[...]).astype(BF16)
    tril = _iota((GMLP_CHUNK, GMLP_CHUNK), 1) <= _iota((GMLP_CHUNK, GMLP_CHUNK), 0)
    zblk = jnp.zeros((GMLP_CHUNK, CH_B), BF16)
    for g2 in range(H_B // 2):
        g0, g1 = 2 * g2, 2 * g2 + 1
        w2 = jnp.concatenate([jnp.where(tril, ws_ref[g0], zero), jnp.where(tril, ws_ref[g1], zero)], axis=1)
        b2 = jnp.concatenate([bsb_ref[g0], bsb_ref[g1]], axis=1)
        cols2 = slice(g0 * CH_B, (g1 + 1) * CH_B)
        for n in range(t_tile // GMLP_CHUNK):
            rows = slice(n * GMLP_CHUNK, (n + 1) * GMLP_CHUNK)
            v2 = vn[rows, cols2]
            vbd = jnp.concatenate([jnp.concatenate([v2[:, :CH_B], zblk], axis=1),
                                   jnp.concatenate([zblk, v2[:, CH_B:]], axis=1)], axis=0)
            ob_ref[rows, cols2] = gu[rows, cols2] * (jnp.dot(w2, vbd, preferred_element_type=F32) + b2)

    hb = _sigmoid(gb_raw) * _dot(ob_ref[...], wpb_ref[...])
    y_ref[0] = _mix_out(x, o_ref[...], og_raw, _sigmoid(ga_raw), hb,
                        hng_ref[...], wpa_ref, wo_ref, gpost_ref[...])

    @pl.when(j == pl.num_programs(1) - 1)
    def _():
        for h in range(H_A):
            hh = h % 2
            sp_ref[0, h] = st_ref[h // 2, hh * DV_A:(hh + 1) * DV_A, hh * DK_A:(hh + 1) * DK_A]


def _const_spec(shape):
    return pl.BlockSpec(shape, lambda *_: (0,) * len(shape), pipeline_mode=pl.Buffered(1))


def _prompt_mixer(x, lbp, gpre, hng, lng, lnb, ws, bsb, gpost, win, wpa, wpb, wo, wup, wdn):
    nb, seq, _ = x.shape
    n_j = seq // T_MIX
    n_step = nb * n_j
    grid = (nb, n_j)
    n_pc = N_PAIR * (T_MIX // HGRN_C)
    consts = (lbp, gpre, hng, lng, lnb, ws, bsb, gpost)
    own = (win, wpa, wpb, wo)
    assert all(w.shape[1] % W_CHUNK == 0 and w.shape[0] <= D_MODEL for w in own)

    def slab_spec(w, steps_per_slab=1):
        rows = w.shape[0] * steps_per_slab // n_step
        assert rows * n_step == w.shape[0] * steps_per_slab and rows % 16 == 0
        return pl.BlockSpec((rows, w.shape[1]), lambda b, j: ((b * n_j + j) // steps_per_slab, 0))

    up_spec, dn_spec = slab_spec(wup), slab_spec(wdn, 2)
    own_specs = [slab_spec(w) for w in own]
    return pl.pallas_call(
        _prompt_mixer_kernel,
        grid=grid,
        in_specs=[pl.BlockSpec((1, T_MIX, D_MODEL), lambda b, j: (b, j, 0))]
        + [_const_spec(w.shape) for w in consts] + [pl.BlockSpec(memory_space=pl.ANY)] * len(own)
        + [up_spec, dn_spec],
        out_specs=(pl.BlockSpec((1, T_MIX, D_MODEL), lambda b, j: (b, j, 0)),
                   pl.BlockSpec((1, H_A, DV_A, DK_A), lambda b, j: (b, 0, 0, 0)),
                   up_spec, dn_spec, *own_specs),
        out_shape=(jax.ShapeDtypeStruct(x.shape, F32),
                   jax.ShapeDtypeStruct((nb, H_A, DV_A, DK_A), F32),
                   jax.ShapeDtypeStruct(wup.shape, BF16),
                   jax.ShapeDtypeStruct(wdn.shape, BF16),
                   *(jax.ShapeDtypeStruct(w.shape, BF16) for w in own)),
        scratch_shapes=[*(pltpu.VMEM(w.shape, BF16) for w in own),
                        pltpu.VMEM((2, D_MODEL, W_CHUNK), F32),
                        pltpu.SemaphoreType.DMA((2,)),
                        pltpu.VMEM((N_PAIR, PAIR_V, PAIR_K), F32),
                        pltpu.VMEM((T_MIX, D_QA), BF16),
                        pltpu.VMEM((T_MIX, D_QA), BF16),
                        pltpu.VMEM((T_MIX, D_QA), BF16),
                        pltpu.VMEM((T_MIX, D_QA), BF16),
                        pltpu.VMEM((T_MIX, D_VA), BF16),
                        pltpu.VMEM((T_MIX // HGRN_C, D_QA), F32),
                        pltpu.VMEM((T_MIX, D_VA), BF16),
                        pltpu.VMEM((n_pc, PAIR_V, PAIR_K), F32),
                        pltpu.VMEM((n_pc, PAIR_K, PAIR_V), BF16),
                        pltpu.VMEM((T_MIX, D_VA), F32),
                        pltpu.VMEM((T_MIX, D_B), F32)],
        compiler_params=pltpu.CompilerParams(dimension_semantics=("arbitrary", "arbitrary"),
                                             vmem_limit_bytes=VMEM_LIMIT),
        name="prompt_mixer",
    )(x, *consts, *own, wup, wdn)


def _prompt_ffn_kernel(x_ref, gpre_ref, wup_ref, cw_ref, cb_ref, wdn_ref, gpost_ref, y_ref, cp_ref,
                       tail_ref, a_ref):
    j = pl.program_id(1)
    t_tile = x_ref.shape[1]

    @pl.when(j == 0)
    def _():
        tail_ref[...] = jnp.zeros_like(tail_ref)

    x = x_ref[0]
    xn = _rms(x, gpre_ref[...]).astype(BF16)
    row = _iota((t_tile, FFN_CB), 0)

    def up_proj(c0):
        return jnp.dot(xn, wup_ref[:, c0:c0 + FFN_CB], preferred_element_type=F32)

    def conv(c0, up):
        cols = slice(c0, c0 + FFN_CB)
        p0 = tail_ref[6:7, cols]
        p1 = tail_ref[7:8, cols]
        tail_ref[:, cols] = up[t_tile - 8:]
        m1 = jnp.where(row == 0, p1, pltpu.roll(up, 1, 0))
        m2 = jnp.where(row == 0, p0, jnp.where(row == 1, p1, pltpu.roll(up, 2, 0)))
        w0, w1, w2 = (cw_ref[k:k + 1, cols].astype(BF16) for k in range(CONV_W))
        return (cb_ref[:, cols].astype(BF16) + w0 * m2.astype(BF16) + w1 * m1.astype(BF16)
                + w2 * up.astype(BF16))

    nblk = D_FF // FFN_CB
    ups = (up_proj(0), up_proj(D_FF))
    for blk in range(nblk):
        c0 = blk * FFN_CB
        nxt = (up_proj(c0 + FFN_CB), up_proj(D_FF + c0 + FFN_CB)) if blk + 1 < nblk else None
        a_ref[:, c0:c0 + FFN_CB] = _gelu_mul(conv(c0, ups[0]), conv(D_FF + c0, ups[1])).astype(BF16)
        ups = nxt

    y_ref[0] = x + _rms(jnp.dot(a_ref[...], wdn_ref[...], preferred_element_type=F32), gpost_ref[...])

    @pl.when(j == pl.num_programs(1) - 1)
    def _():
        cp_ref[0] = tail_ref[6:8, :]


def _prompt_ffn(x, gpre, wup, cw, cb, wdn, gpost):
    nb, seq, _ = x.shape
    weights = (gpre, wup, cw, cb, wdn, gpost)
    return pl.pallas_call(
        _prompt_ffn_kernel,
        grid=(nb, seq // T_FFN),
        in_specs=[pl.BlockSpec((1, T_FFN, D_MODEL), lambda b, j: (b, j, 0))]
        + [_const_spec(w.shape) for w in weights],
        out_specs=(pl.BlockSpec((1, T_FFN, D_MODEL), lambda b, j: (b, j, 0)),
                   pl.BlockSpec((1, CONV_W - 1, 2 * D_FF), lambda b, j: (b, 0, 0))),
        out_shape=(jax.ShapeDtypeStruct(x.shape, F32),
                   jax.ShapeDtypeStruct((nb, CONV_W - 1, 2 * D_FF), F32)),
        scratch_shapes=[pltpu.VMEM((8, 2 * D_FF), F32),
                        pltpu.VMEM((T_FFN, D_FF), BF16)],
        compiler_params=pltpu.CompilerParams(dimension_semantics=("arbitrary", "arbitrary"),
                                             vmem_limit_bytes=VMEM_LIMIT),
        name="prompt_ffn",
    )(x, *weights)


def _sample_pre_kernel(wsm_ref, bsm_ref, x_ref, lbp_ref, gpre_ref, win_ref, lng_ref, lnb_ref, wpb_ref,
                       qi_ref, kh_ref, d_ref, i_ref, oin_ref, og_ref, gas_ref, hb_ref, vn_ref):
    n_rows = x_ref.shape[0]
    seq = 4
    x = x_ref[...]
    xn = _rms(x, gpre_ref[...]).astype(BF16)
    lb = _lower_bound(lbp_ref[...])

    def proj(off, width):
        return jnp.dot(xn, win_ref[:, off:off + width], preferred_element_type=F32)

    def tok(width):
        return _mod(_iota((n_rows, width), 0), seq)

    def shift(v, k):
        return pltpu.roll(v, k, 0) if k else v

    qf, logf, kk = _hgrn_features(proj(OFF_Q, D_QA), proj(OFF_F, D_QA), lb)
    iv = proj(OFF_I, D_VA)
    tq = tok(D_QA)
    b = logf + jnp.where(tq >= 1, shift(logf, 1), 0.0)
    b = b + jnp.where(tq >= 2, shift(b, 2), 0.0)
    b_last = jnp.where(tq == seq - 1, b, 0.0)
    b_last = b_last + pltpu.roll(b_last, n_rows - 1, 0)
    b_last = b_last + pltpu.roll(b_last, n_rows - 2, 0)
    qi_ref[...] = qf * jnp.exp(b)
    kh_ref[...] = kk * jnp.exp(b_last - b)
    d_ref[...] = jnp.exp(b_last)
    i_ref[...] = iv

    head_sum = _ones_where(_div(_iota((D_QA, D_VA), 0), DK_A) == _div(_iota((D_QA, D_VA), 1), DV_A))
    tv = tok(D_VA)
    o = jnp.zeros((n_rows, D_VA), F32)
    for k in range(seq):
        pair = qf * shift(kk, k) * jnp.exp(b - shift(b, k))
        sc = jnp.dot(jnp.where(tq >= k, pair, 0.0).astype(BF16), head_sum, preferred_element_type=F32)
        o = o + jnp.where(tv >= k, sc * shift(iv, k), 0.0)
    oin_ref[...] = o

    gu = _gelu(proj(OFF_U, D_B))
    vn = _layer_norm(_gelu(proj(OFF_V, D_B)), lng_ref[...], lnb_ref[...])
    vn_ref[...] = vn
    tg = tok(CH_B)
    obs = []
    for g in range(H_B):
        vg = vn[:, g * CH_B:(g + 1) * CH_B]
        s = jnp.zeros((n_rows, CH_B), F32)
        for t in range(seq):
            s = s + jnp.where(tg == t, bsm_ref[g * seq + t], 0.0)
        for k in range(seq):
            coef = jnp.zeros((n_rows, CH_B), F32)
            for t in range(k, seq):
                coef = coef + jnp.where(tg == t, wsm_ref[(g * seq + t) * seq + (t - k)], 0.0)
            s = s + coef * shift(vg, k)
        obs.append(gu[:, g * CH_B:(g + 1) * CH_B] * s)
    ob = jnp.concatenate(obs, axis=1)
    hb_ref[...] = _sigmoid(proj(OFF_GB, D_MODEL)) * _dot(ob, wpb_ref[...])
    gas_ref[...] = _sigmoid(proj(OFF_GA, D_MODEL))
    og_ref[...] = proj(OFF_OG, D_VA)


def _sample_pre(wsm, bsm, x, lbp, gpre, win, lng, lnb, wpb):
    n_rows = x.shape[0]
    shapes = [(n_rows, D_QA), (n_rows, D_QA), (n_rows, D_QA), (n_rows, D_VA), (n_rows, D_VA),
              (n_rows, D_VA), (n_rows, D_MODEL), (n_rows, D_MODEL), (n_rows, D_B)]
    smem = pl.BlockSpec(memory_space=pltpu.SMEM)
    return pl.pallas_call(
        _sample_pre_kernel,
        in_specs=[smem, smem] + [pl.BlockSpec(memory_space=pltpu.VMEM)] * 7,
        out_shape=tuple(jax.ShapeDtypeStruct(s, F32) for s in shapes),
        compiler_params=pltpu.CompilerParams(vmem_limit_bytes=VMEM_LIMIT),
        name="sample_pre",
    )(wsm, bsm, x, lbp, gpre, win, lng, lnb, wpb)


def _sample_state_kernel(qi_ref, kh_ref, d_ref, i_ref, s_ref, o_ref, sn_ref):
    seq = 4
    nseq = 4
    blk = nseq * seq
    unroll = 2
    rows_seq = _div(_iota((blk, DV_A), 0), seq)

    @pl.loop(0, qi_ref.shape[0] // (blk * unroll))
    def _(trip):
        units = [(u, h) for u in range(unroll) for h in range(H_A)]

        def rows_of(u):
            return pl.ds(pl.multiple_of((trip * unroll + u) * blk, blk), blk)

        def state_index(u, h, bb):
            return ((trip * unroll + u) * nseq + bb) * H_A + h

        def kcols(h):
            return slice(h * DK_A, (h + 1) * DK_A)

        reads = {}
        for u, h in units:
            stacked = jnp.concatenate([s_ref[state_index(u, h, bb)] for bb in range(nseq)], axis=0)
            reads[u, h] = lax.dot_general(qi_ref[rows_of(u), kcols(h)].astype(BF16), stacked.astype(BF16),
                                          (((1,), (1,)), ((), ())), preferred_element_type=F32)
        incs = {}
        for u, h in units:
            i16 = i_ref[rows_of(u), h * DV_A:(h + 1) * DV_A]
            i_sel = jnp.concatenate([jnp.where(rows_seq == bb, i16, 0.0) for bb in range(nseq)], axis=1)
            incs[u, h] = _dot_tn(i_sel, kh_ref[rows_of(u), kcols(h)])
        for u in range(unroll):
            outs = []
            for h in range(H_A):
                r = reads[u, h]
                oh = r[:, :DV_A]
                for bb in range(1, nseq):
                    oh = jnp.where(rows_seq == bb, r[:, bb * DV_A:(bb + 1) * DV_A], oh)
                outs.append(oh)
                d16 = d_ref[rows_of(u), kcols(h)]
                for bb in range(nseq):
                    i = state_index(u, h, bb)
                    sn_ref[i] = s_ref[i] * d16[bb * seq:bb * seq + 1] + incs[u, h][bb * DV_A:(bb + 1) * DV_A]
            o_ref[rows_of(u), :] = jnp.concatenate(outs, axis=1)


def _sample_state(qi, kh, d, iv, state):
    n_rows = qi.shape[0]
    rb = SAMPLE_BB * 4
    sb = SAMPLE_BB * H_A
    return pl.pallas_call(
        _sample_state_kernel,
        grid=(n_rows // rb,),
        in_specs=[pl.BlockSpec((rb, D_QA), lambda i: (i, 0)),
                  pl.BlockSpec((rb, D_QA), lambda i: (i, 0)),
                  pl.BlockSpec((rb, D_QA), lambda i: (i, 0)),
                  pl.BlockSpec((rb, D_VA), lambda i: (i, 0)),
                  pl.BlockSpec((sb, DV_A, DK_A), lambda i: (i, 0, 0))],
        out_specs=(pl.BlockSpec((rb, D_VA), lambda i: (i, 0)),
                   pl.BlockSpec((sb, DV_A, DK_A), lambda i: (i, 0, 0))),
        out_shape=(jax.ShapeDtypeStruct((n_rows, D_VA), F32),
                   jax.ShapeDtypeStruct(state.shape, F32)),
        compiler_params=pltpu.CompilerParams(dimension_semantics=("arbitrary",),
                                             vmem_limit_bytes=VMEM_LIMIT),
        name="sample_state",
    )(qi, kh, d, iv, state)


def _sample_post_kernel(x_ref, oin_ref, oint_ref, og_ref, gas_ref, hb_ref, hng_ref, wpa_ref, wo_ref,
                        gpost_ref, y_ref):
    y_ref[...] = _mix_out(x_ref[...], oin_ref[...] + oint_ref[...], og_ref[...], gas_ref[...], hb_ref[...],
                          hng_ref[...], wpa_ref, wo_ref, gpost_ref[...])


def _sample_post(x, oin, oint, og, gas, hb, hng, wpa, wo, gpost):
    return pl.pallas_call(
        _sample_post_kernel,
        out_shape=jax.ShapeDtypeStruct(x.shape, F32),
        compiler_params=pltpu.CompilerParams(vmem_limit_bytes=VMEM_LIMIT),
        name="sample_post",
    )(x, oin, oint, og, gas, hb, hng, wpa, wo, gpost)


def _sample_ffn_kernel(x_ref, gpre_ref, wu_ref, cache_ref, cw_ref, cb_ref, wdn_ref, gpost_ref, y_ref, cs_ref,
                       xn_ref, acc_ref, old_ref, up_ref, val_ref):
    step = pl.program_id(0)
    nblk = pl.num_programs(0) // 2
    n_rows = x_ref.shape[0]
    n_seq = cache_ref.shape[0]
    seq = n_rows // n_seq
    keep = CONV_W - 1

    @pl.when(step == 0)
    def _():
        xn_ref[...] = _rms(x_ref[...], gpre_ref[...]).astype(BF16)
        acc_ref[...] = jnp.zeros_like(acc_ref)
        old_ref[...] = jnp.zeros_like(old_ref)

    up = jnp.dot(xn_ref[...], wu_ref[...], preferred_element_type=F32)
    up_ref[...] = up
    for b in range(n_seq):
        old_ref[seq * b:seq * b + keep, :] = cache_ref[b]
        cs_ref[b] = up_ref[seq * (b + 1) - keep:seq * (b + 1), :]
    old = old_ref[...]
    tok = _mod(_iota(up.shape, 0), seq)
    m2 = jnp.where(tok >= 2, pltpu.roll(up, 2, 0), old)
    m1 = jnp.where(tok >= 1, pltpu.roll(up, 1, 0), pltpu.roll(old, n_rows - 1, 0))
    conv = cb_ref[...] + cw_ref[0:1, :] * m2 + cw_ref[1:2, :] * m1 + cw_ref[2:3, :] * up

    @pl.when(step < nblk)
    def _():
        val_ref[step] = conv

    @pl.when(step >= nblk)
    def _():
        a = _gelu_mul(conv, val_ref[step - nblk])
        acc_ref[...] += _dot(a, wdn_ref[...])

    @pl.when(step == pl.num_programs(0) - 1)
    def _():
        y_ref[...] = x_ref[...] + _rms(acc_ref[...], gpost_ref[...])


def _sample_ffn(x, gpre, wup, cache, cw, cb, wdn, gpost):
    n_rows = x.shape[0]
    n_seq = cache.shape[0]
    nblk = D_FF // SAMPLE_FFN_CB
    col = lambda i: (0, lax.rem(i + nblk, 2 * nblk))
    col3 = lambda i: (0, 0, lax.rem(i + nblk, 2 * nblk))
    full = lambda shape: pl.BlockSpec(shape, lambda i: (0,) * len(shape))
    return pl.pallas_call(
        _sample_ffn_kernel,
        grid=(2 * nblk,),
        in_specs=[full((n_rows, D_MODEL)), full((1, D_MODEL)),
                  pl.BlockSpec((D_MODEL, SAMPLE_FFN_CB), col),
                  pl.BlockSpec((n_seq, CONV_W - 1, SAMPLE_FFN_CB), col3),
                  pl.BlockSpec((CONV_W, SAMPLE_FFN_CB), col),
                  pl.BlockSpec((1, SAMPLE_FFN_CB), col),
                  pl.BlockSpec((SAMPLE_FFN_CB, D_MODEL), lambda i: (jnp.maximum(i - nblk, 0), 0)),
                  full((1, D_MODEL))],
        out_specs=(full((n_rows, D_MODEL)),
                   pl.BlockSpec((n_seq, CONV_W - 1, SAMPLE_FFN_CB), col3)),
        out_shape=(jax.ShapeDtypeStruct((n_rows, D_MODEL), F32),
                   jax.ShapeDtypeStruct(cache.shape, F32)),
        scratch_shapes=[pltpu.VMEM((n_rows, D_MODEL), BF16), pltpu.VMEM((n_rows, D_MODEL), F32),
                        pltpu.VMEM((n_rows, SAMPLE_FFN_CB), F32), pltpu.VMEM((n_rows, SAMPLE_FFN_CB), F32),
                        pltpu.VMEM((nblk, n_rows, SAMPLE_FFN_CB), F32)],
        compiler_params=pltpu.CompilerParams(dimension_semantics=("arbitrary",),
                                             vmem_limit_bytes=VMEM_LIMIT),
        name="sample_ffn",
    )(x, gpre, wup, cache, cw, cb, wdn, gpost)


def kernel(x_prompt, x_sample, state_hgrn, cache_ffn_conv, lb_param, mix_pre_g, w_in, hgrn_norm_g, gmlp_ln_g, gmlp_ln_b, w_s, b_s, w_pa, w_pb, w_o, mix_post_g, ffn_pre_g, w_up, conv_w, conv_b, w_down, ffn_post_g):
    nb_s, seq_s, _ = x_sample.shape
    row = lambda v: v.reshape(1, -1)
    ws = w_s[0].astype(BF16)
    gpre, gpost, fpre, fpost = row(mix_pre_g[0]), row(mix_post_g[0]), row(ffn_pre_g[0]), row(ffn_post_g[0])
    lng, lnb, cb = row(gmlp_ln_g[0]), row(gmlp_ln_b[0]), row(conv_b[0])
    hng = row(jnp.tile(hgrn_norm_g[0], H_A))
    bsb = jnp.broadcast_to(b_s[0][:, :, None], (H_B, GMLP_CHUNK, CH_B))

    x1, sp, wup, wdn, win, wpa, wpb, wo = _prompt_mixer(
        x_prompt, lb_param, gpre, hng, lng, lnb, ws, bsb, gpost, w_in[0], w_pa[0], w_pb[0], w_o[0], w_up[0], w_down[0])
    yp, cp = _prompt_ffn(x1, fpre, wup, conv_w[0], cb, wdn, fpost)

    xs = x_sample.reshape(nb_s * seq_s, D_MODEL)
    wsm = w_s[0][:, :seq_s, :seq_s].reshape(-1)
    bsm = b_s[0][:, :seq_s].reshape(-1)
    qi, kh, d, iv, oin, og, gas, hb, vn = _sample_pre(wsm, bsm, xs, lb_param, gpre, win, lng, lnb, wpb)
    st_in = jnp.swapaxes(state_hgrn[0], -1, -2).reshape(nb_s * H_A, DV_A, DK_A)
    oint, ss = _sample_state(qi, kh, d, iv, st_in)
    xs1 = _sample_post(xs, oin, oint, og, gas, hb, hng, wpa, wo, gpost)
    ys, cs = _sample_ffn(xs1, fpre, wup, cache_ffn_conv[0], conv_w[0], cb, wdn, fpost)

    return (yp, ys.reshape(nb_s, seq_s, D_MODEL), jnp.swapaxes(sp, -1, -2)[None],
            jnp.swapaxes(ss.reshape(1, nb_s, H_A, DV_A, DK_A), -1, -2), cp[None], cs[None],
            vn.reshape(1, nb_s, seq_s, D_B))
```
